```python
import math
import jax
import jax.numpy as jnp
from jax import lax
import numpy as np

D_MODEL = 2048
BATCH = 2
SEQ = 4096
DEPTH = 4
DEC_BATCH = 8
DEC_SEQ = 4
PAST_LEN = 16384
PAGE_SIZE = 128

N_PAIR = DEPTH // 2
HEAD_DIM = 128
A_WIDTH = D_MODEL // 2
A_HEADS = A_WIDTH // HEAD_DIM
IDX_HEADS = 16
IDX_DIM = 64
IDX_TOPK = 256
A_Q_BLOCK = 64
B_WIDTH = D_MODEL - A_WIDTH
B_GROUPS = 16
CONV_W = 3
C_WIDTH = D_MODEL // 2
S5_GROUP = 16
S5_NGROUPS = C_WIDTH // S5_GROUP
S5_STATE = 64
D_WIDTH = D_MODEL - C_WIDTH
D_HEADS = D_WIDTH // HEAD_DIM
MOBA_BLOCK = 256
MOBA_TOPK = 3
D_Q_BLOCK = 16
N_GROUPS = 4
EXP_PER_GROUP = 8
N_EXPERTS = N_GROUPS * EXP_PER_GROUP
EXPERT_FF = 256
MOE_TOPK = 2
EPS = 1e-6
AB_SIZES = (A_WIDTH, A_WIDTH, A_WIDTH, IDX_HEADS * IDX_DIM, IDX_DIM, IDX_HEADS, B_WIDTH, B_WIDTH, B_WIDTH)
CD_SIZES = (C_WIDTH, D_WIDTH, D_WIDTH, D_WIDTH)
F32 = jnp.float32

kernel_name = 'hybrid_dsa_conv_s5_moba_hmoe_step'


def split_cols(x, sizes):
    cuts = np.cumsum(sizes)[:-1].tolist()
    return jnp.split(x, cuts, axis=-1)


def rmsnorm(x, g):
    xf = x.astype(F32)
    y = xf * lax.rsqrt(jnp.mean(xf * xf, axis=-1, keepdims=True) + EPS)
    return (y * g.astype(F32)).astype(x.dtype)


def to_blocks(a, qb):
    b, t = a.shape[:2]
    return jnp.moveaxis(a.reshape((b, t // qb, qb) + a.shape[2:]), 1, 0)


def from_blocks(a):
    nb, b, qb = a.shape[:3]
    return jnp.moveaxis(a, 0, 1).reshape((b, nb * qb) + a.shape[3:])


def attend(q, ks, vs, valid):
    s = jnp.einsum('bqhd,bqhkd->bqhk', q, ks, preferred_element_type=F32) * HEAD_DIM ** -0.5
    s = jnp.where(valid, s, -jnp.inf)
    pr = jax.nn.softmax(s, axis=-1).astype(vs.dtype)
    return jnp.einsum('bqhk,bqhkd->bqhd', pr, vs)


def paged_rows(pool, layer, page_table, new, pos, head):
    n_past = page_table.shape[1] * PAGE_SIZE
    bidx = jnp.arange(pos.shape[0]).reshape((-1,) + (1,) * (pos.ndim - 1))
    past_pos = jnp.clip(pos, 0, n_past - 1)
    phys = page_table[bidx, past_pos // PAGE_SIZE]
    from_pool = pool[layer, phys, past_pos % PAGE_SIZE, head]
    new_pos = jnp.clip(pos - n_past, 0, new.shape[1] - 1)
    from_new = new[bidx, new_pos, head].astype(from_pool.dtype)
    return jnp.where((pos < n_past)[..., None], from_pool, from_new)


def indexer_select(iq, iw, ik, pos, n_sel):
    s = jnp.einsum('bqhd,bsd->bqhs', iq, ik, preferred_element_type=F32) * IDX_DIM ** -0.5
    score = jnp.einsum('bqhs,bqh->bqs', jax.nn.relu(s), iw.astype(F32)) * IDX_HEADS ** -0.5
    causal = jnp.arange(ik.shape[1])[None, :] <= pos[:, None]
    score = jnp.where(causal[None], score, -jnp.inf)
    _, idx = lax.top_k(score, n_sel)
    return idx, idx <= pos[None, :, None]


def dsa_prompt(q, k, v, iq, ik, iw):
    Bn, T = q.shape[:2]
    n_sel = min(IDX_TOPK, T // 4)
    bidx = jnp.arange(Bn)[:, None, None, None]
    hidx = jnp.arange(A_HEADS)[None, None, :, None]

    def one(args):
        start, qb, iqb, iwb = args
        pos = start + jnp.arange(A_Q_BLOCK)
        idx, valid = indexer_select(iqb, iwb, ik, pos, n_sel)
        sel = idx[:, :, None, :]
        return attend(qb, k[bidx, sel, hidx], v[bidx, sel, hidx], valid[:, :, None, :])

    starts = jnp.arange(T // A_Q_BLOCK) * A_Q_BLOCK
    out = lax.map(one, (starts, to_blocks(q, A_Q_BLOCK), to_blocks(iq, A_Q_BLOCK), to_blocks(iw, A_Q_BLOCK)))
    return from_blocks(out)


def dsa_sample(q, k, v, iq, ik, iw, pool_k, pool_v, pool_kidx, layer, page_table, pos):
    Bn = q.shape[0]
    n_past = page_table.shape[1] * PAGE_SIZE
    ik_past = pool_kidx[layer, page_table].reshape(Bn, n_past, IDX_DIM)
    ik_all = jnp.concatenate([ik_past, ik.astype(ik_past.dtype)], axis=1)
    n_sel = min(IDX_TOPK, ik_all.shape[1] // 4)
    idx, valid = indexer_select(iq, iw, ik_all, pos, n_sel)
    hidx = jnp.arange(A_HEADS)[None, None, :, None]
    sel = idx[:, :, None, :]
    ks = paged_rows(pool_k, layer, page_table, k, sel, hidx)
    vs = paged_rows(pool_v, layer, page_table, v, sel, hidx)
    return attend(q, ks, vs, valid[:, :, None, :])


def short_conv(gb, gc, gh, buf, w):
    u = gc * gh
    T = u.shape[1]
    xin = jnp.concatenate([buf.astype(u.dtype), u], axis=1)
    y = sum(w[j] * xin[:, j:j + T] for j in range(CONV_W))
    return gb * y, xin[:, -(CONV_W - 1):]


def _cplx_combine(e1, e2):
    a1r, a1i, b1r, b1i = e1
    a2r, a2i, b2r, b2i = e2
    return (a2r * a1r - a2i * a1i, a2r * a1i + a2i * a1r,
            a2r * b1r - a2i * b1i + b2r, a2r * b1i + a2i * b1r + b2i)


def s5(u, h0_re, h0_im, i, p):
    Bn, T, _ = u.shape
    uf = u.astype(F32).reshape(Bn, T, S5_NGROUPS, S5_GROUP)
    dt = jnp.exp(p['s5_log_dt'][i].astype(F32))[:, None]
    lr = p['s5_a_re'][i].astype(F32)
    li = p['s5_a_im'][i].astype(F32)
    mag = jnp.exp(lr * dt)
    ab_re = mag * jnp.cos(li * dt)
    ab_im = mag * jnp.sin(li * dt)
    den = lr * lr + li * li
    f_re = ((ab_re - 1.0) * lr + ab_im * li) / den
    f_im = (ab_im * lr - (ab_re - 1.0) * li) / den
    br = p['s5_b_re'][i].astype(F32)
    bi = p['s5_b_im'][i].astype(F32)
    bb_re = f_re[..., None] * br - f_im[..., None] * bi
    bb_im = f_re[..., None] * bi + f_im[..., None] * br
    bu_re = jnp.einsum('btgi,gpi->btgp', uf, bb_re)
    bu_im = jnp.einsum('btgi,gpi->btgp', uf, bb_im)
    shape = (1, T, S5_NGROUPS, S5_STATE)
    acr, aci, sr, si = lax.associative_scan(
        _cplx_combine, (jnp.broadcast_to(ab_re, shape), jnp.broadcast_to(ab_im, shape), bu_re, bu_im), axis=1)
    h0r = h0_re.astype(F32)[:, None]
    h0i = h0_im.astype(F32)[:, None]
    h_re = acr * h0r - aci * h0i + sr
    h_im = acr * h0i + aci * h0r + si
    cr = p['s5_c_re'][i].astype(F32)
    ci = p['s5_c_im'][i].astype(F32)
    y = jnp.einsum('gip,btgp->btgi', cr, h_re) - jnp.einsum('gip,btgp->btgi', ci, h_im)
    y = y.reshape(Bn, T, C_WIDTH) + p['s5_d'][i].astype(F32) * uf.reshape(Bn, T, C_WIDTH)
    z = jax.nn.gelu(y)
    out = z * jax.nn.sigmoid(z @ p['glu_w'][i].astype(F32) + p['glu_b'][i].astype(F32))
    return out.astype(u.dtype), h_re[:, -1], h_im[:, -1]


def moba_select(q, k_mean, pos):
    nb = k_mean.shape[1]
    s = jnp.einsum('bqhd,bnhd->bqhn', q.astype(F32), k_mean.astype(F32))
    own = pos // MOBA_BLOCK
    fully_past = jnp.arange(nb)[None, :] < own[:, None]
    s = jnp.where(fully_past[None, :, None, :], s, -jnp.inf)
    n_top = min(MOBA_TOPK, nb)
    _, top = lax.top_k(s, n_top)
    top_ok = top < own[None, :, None, None]
    own_b = jnp.broadcast_to(own[None, :, None, None], top.shape[:3] + (1,)).astype(top.dtype)
    blk = jnp.concatenate([top, own_b], axis=-1)
    ok = jnp.concatenate([top_ok, jnp.ones(top.shape[:3] + (1,), bool)], axis=-1)
    key_pos = (blk[..., None] * MOBA_BLOCK + jnp.arange(MOBA_BLOCK)).reshape(blk.shape[:3] + (-1,))
    valid = jnp.repeat(ok, MOBA_BLOCK, axis=-1) & (key_pos <= pos[None, :, None, None])
    return key_pos, valid


def moba_prompt(q, k, v):
    Bn, T = q.shape[:2]
    nb = -(-T // MOBA_BLOCK)
    pad = ((0, 0), (0, nb * MOBA_BLOCK - T), (0, 0), (0, 0))
    kp = jnp.pad(k, pad)
    vp = jnp.pad(v, pad)
    k_mean = kp.reshape(Bn, nb, MOBA_BLOCK, D_HEADS, HEAD_DIM).astype(F32).mean(axis=2)
    bidx = jnp.arange(Bn)[:, None, None, None]
    hidx = jnp.arange(D_HEADS)[None, None, :, None]

    def one(args):
        start, qb = args
        pos = start + jnp.arange(D_Q_BLOCK)
        key_pos, valid = moba_select(qb, k_mean, pos)
        return attend(qb, kp[bidx, key_pos, hidx], vp[bidx, key_pos, hidx], valid)

    starts = jnp.arange(T // D_Q_BLOCK) * D_Q_BLOCK
    return from_blocks(lax.map(one, (starts, to_blocks(q, D_Q_BLOCK))))


def moba_sample(q, k, v, pool_k, pool_v, layer, page_table, pos):
    T = q.shape[1]
    n_pages = page_table.shape[1]
    n_past = n_pages * PAGE_SIZE
    nb = -(-(n_past + T) // MOBA_BLOCK)
    page_sum = pool_k[layer, page_table].astype(F32).sum(axis=2)
    page_blk = (jnp.arange(n_pages) * PAGE_SIZE) // MOBA_BLOCK
    new_blk = (n_past + jnp.arange(T)) // MOBA_BLOCK
    blk_sum = (jax.ops.segment_sum(jnp.moveaxis(page_sum, 1, 0), page_blk, num_segments=nb)
               + jax.ops.segment_sum(jnp.moveaxis(k.astype(F32), 1, 0), new_blk, num_segments=nb))
    blk_cnt = (jax.ops.segment_sum(jnp.full((n_pages,), PAGE_SIZE, F32), page_blk, num_segments=nb)
               + jax.ops.segment_sum(jnp.ones((T,), F32), new_blk, num_segments=nb))
    k_mean = jnp.moveaxis(blk_sum / blk_cnt[:, None, None, None], 0, 1)
    key_pos, valid = moba_select(q, k_mean, pos)
    hidx = jnp.arange(D_HEADS)[None, None, :, None]
    ks = paged_rows(pool_k, layer, page_table, k, key_pos, hidx)
    vs = paged_rows(pool_v, layer, page_table, v, key_pos, hidx)
    return attend(q, ks, vs, valid)


def mixer_ab(h, i, p, past, pos):
    Bn, T, _ = h.shape
    q, k, v, iq, ik, iw, gb, gc, gh = split_cols(h @ p['w_in_ab'][i], AB_SIZES)
    hs = (Bn, T, A_HEADS, HEAD_DIM)
    q, k, v = q.reshape(hs), k.reshape(hs), v.reshape(hs)
    iq = iq.reshape(Bn, T, IDX_HEADS, IDX_DIM)
    if past is None:
        a = dsa_prompt(q, k, v, iq, ik, iw)
        buf = jnp.zeros((Bn, CONV_W - 1, B_WIDTH), h.dtype)
    else:
        a = dsa_sample(q, k, v, iq, ik, iw, past['cache_a_k'], past['cache_a_v'], past['cache_a_kidx'], i,
                       past['page_table'], pos)
        buf = past['state_b_conv'][i]
    b, new_buf = short_conv(gb, gc, gh, buf, p['conv_w'][i])
    y = jnp.concatenate([a.reshape(Bn, T, A_WIDTH), b], axis=-1) @ p['w_out_ab'][i]
    return y, (k, v, ik, new_buf)


def mixer_cd(h, i, p, past, pos):
    Bn, T, _ = h.shape
    u, q, k, v = split_cols(h @ p['w_in_cd'][i], CD_SIZES)
    hs = (Bn, T, D_HEADS, HEAD_DIM)
    q, k, v = q.reshape(hs), k.reshape(hs), v.reshape(hs)
    if past is None:
        h0r = jnp.zeros((Bn, S5_NGROUPS, S5_STATE), F32)
        h0i = jnp.zeros((Bn, S5_NGROUPS, S5_STATE), F32)
        d = moba_prompt(q, k, v)
    else:
        h0r = past['state_c_re'][i]
        h0i = past['state_c_im'][i]
        d = moba_sample(q, k, v, past['cache_d_k'], past['cache_d_v'], i, past['page_table'], pos)
    c, hr, hi = s5(u, h0r, h0i, i, p)
    y = jnp.concatenate([c, d.reshape(Bn, T, D_WIDTH)], axis=-1) @ p['w_out_cd'][i]
    return y, (hr, hi, k, v)


def hier_moe(h, l, p):
    Bn, T, D = h.shape
    hf = h.reshape(Bn * T, D)
    g_logit = (hf @ p['router_g_w'][l] + p['router_g_b'][l]).astype(F32)
    g_sel = jnp.argmax(g_logit, axis=-1)
    g_gate = jnp.take_along_axis(jax.nn.softmax(g_logit, axis=-1), g_sel[:, None], axis=-1)
    e_logit = (hf @ p['router_e_w'][l] + p['router_e_b'][l]).astype(F32).reshape(-1, N_GROUPS, EXP_PER_GROUP)
    e_logit = jnp.take_along_axis(e_logit, g_sel[:, None, None], axis=1)[:, 0]
    top_v, top_i = lax.top_k(e_logit, MOE_TOPK)
    top_w = jax.nn.softmax(top_v, axis=-1) * g_gate
    expert_id = g_sel[:, None] * EXP_PER_GROUP + top_i
    comb = jnp.sum(jax.nn.one_hot(expert_id, N_EXPERTS, dtype=F32) * top_w[..., None], axis=1).astype(h.dtype)
    out = jnp.zeros_like(hf)
    for g in range(N_GROUPS):
        e = slice(g * EXP_PER_GROUP, (g + 1) * EXP_PER_GROUP)
        a = jnp.einsum('nd,edf->nef', hf, p['exp_w1'][l, e])
        b = jnp.einsum('nd,edf->nef', hf, p['exp_w3'][l, e])
        out = out + jnp.einsum('nef,efd->nd', jax.nn.silu(a) * b * comb[:, e, None], p['exp_w2'][l, e])
    return out.reshape(Bn, T, D)


def trunk(x, c, p, past):
    Bn, T, _ = x.shape
    start = 0 if past is None else past['page_table'].shape[1] * PAGE_SIZE
    pos = start + jnp.arange(T)
    cs = jax.nn.silu(c)
    ab_states, cd_states = [], []
    for l in range(DEPTH):
        mod = cs @ p['ada_w'][l] + p['ada_b'][l]
        sh1, sc1, g1, sh2, sc2, g2 = [m[:, None, :] for m in jnp.split(mod, 6, axis=-1)]
        hn = rmsnorm(x, p['norm1'][l]) * (1.0 + sc1) + sh1
        if l % 2 == 0:
            y, st = mixer_ab(hn, l // 2, p, past, pos)
            ab_states.append(st)
        else:
            y, st = mixer_cd(hn, l // 2, p, past, pos)
            cd_states.append(st)
        x = x + g1 * y
        hn = rmsnorm(x, p['norm2'][l]) * (1.0 + sc2) + sh2
        x = x + g2 * hier_moe(hn, l, p)
    a_k, a_v, a_kidx, b_conv = [jnp.stack(s) for s in zip(*ab_states)]
    c_re, c_im, d_k, d_v = [jnp.stack(s) for s in zip(*cd_states)]
    return rmsnorm(x, p['final_norm']), a_k, a_v, a_kidx, b_conv, c_re, c_im, d_k, d_v


def setup_inputs(seed: int = 0) -> dict:
    key = jax.random.key(seed)
    keys = iter(jax.random.split(key, 48))

    def nrm(shape, scale):
        return jax.random.normal(next(keys), shape, F32) * scale

    n_pages = PAST_LEN // PAGE_SIZE
    n_used = DEC_BATCH * n_pages
    n_pool = n_used + max(1, n_used // 4)
    dsc = D_MODEL ** -0.5
    page_table = jax.random.permutation(next(keys), n_pool)[:n_used].reshape(DEC_BATCH, n_pages).astype(jnp.int32)
    return {
        'x_prompt': nrm((BATCH, SEQ, D_MODEL), 1.0),
        'x_sample': nrm((DEC_BATCH, DEC_SEQ, D_MODEL), 1.0),
        'cache_a_k': nrm((N_PAIR, n_pool, PAGE_SIZE, A_HEADS, HEAD_DIM), 1.0),
        'cache_a_v': nrm((N_PAIR, n_pool, PAGE_SIZE, A_HEADS, HEAD_DIM), 1.0),
        'cache_a_kidx': nrm((N_PAIR, n_pool, PAGE_SIZE, IDX_DIM), 1.0),
        'state_b_conv': nrm((N_PAIR, DEC_BATCH, CONV_W - 1, B_WIDTH), 0.5),
        'state_c_re': nrm((N_PAIR, DEC_BATCH, S5_NGROUPS, S5_STATE), 0.1),
        'state_c_im': nrm((N_PAIR, DEC_BATCH, S5_NGROUPS, S5_STATE), 0.1),
        'cache_d_k': nrm((N_PAIR, n_pool, PAGE_SIZE, D_HEADS, HEAD_DIM), 1.0),
        'cache_d_v': nrm((N_PAIR, n_pool, PAGE_SIZE, D_HEADS, HEAD_DIM), 1.0),
        'page_table': page_table,
        'c_prompt': nrm((BATCH, D_MODEL), 1.0),
        'c_sample': nrm((DEC_BATCH, D_MODEL), 1.0),
        'norm1': 1.0 + nrm((DEPTH, D_MODEL), 0.05),
        'norm2': 1.0 + nrm((DEPTH, D_MODEL), 0.05),
        'ada_w': nrm((DEPTH, D_MODEL, 6 * D_MODEL), 0.5 * dsc),
        'ada_b': nrm((DEPTH, 6 * D_MODEL), 0.1),
        'w_in_ab': nrm((N_PAIR, D_MODEL, sum(AB_SIZES)), dsc),
        'conv_w': nrm((N_PAIR, CONV_W, B_WIDTH), CONV_W ** -0.5),
        'w_out_ab': nrm((N_PAIR, A_WIDTH + B_WIDTH, D_MODEL), (A_WIDTH + B_WIDTH) ** -0.5),
        'w_in_cd': nrm((N_PAIR, D_MODEL, sum(CD_SIZES)), dsc),
        's5_a_re': -0.5 + nrm((N_PAIR, S5_NGROUPS, S5_STATE), 0.01),
        's5_a_im': jnp.pi * jnp.arange(S5_STATE, dtype=F32) + nrm((N_PAIR, S5_NGROUPS, S5_STATE), 0.01),
        's5_b_re': nrm((N_PAIR, S5_NGROUPS, S5_STATE, S5_GROUP), (2 * S5_GROUP) ** -0.5),
        's5_b_im': nrm((N_PAIR, S5_NGROUPS, S5_STATE, S5_GROUP), (2 * S5_GROUP) ** -0.5),
        's5_c_re': nrm((N_PAIR, S5_NGROUPS, S5_GROUP, S5_STATE), S5_STATE ** -0.5),
        's5_c_im': nrm((N_PAIR, S5_NGROUPS, S5_GROUP, S5_STATE), S5_STATE ** -0.5),
        's5_d': nrm((N_PAIR, C_WIDTH), 1.0),
        's5_log_dt': jax.random.uniform(next(keys), (N_PAIR, S5_NGROUPS), F32,
                                        minval=math.log(1e-3), maxval=math.log(1e-1)),
        'glu_w': nrm((N_PAIR, C_WIDTH, C_WIDTH), C_WIDTH ** -0.5),
        'glu_b': nrm((N_PAIR, C_WIDTH), 0.01),
        'w_out_cd': nrm((N_PAIR, C_WIDTH + D_WIDTH, D_MODEL), (C_WIDTH + D_WIDTH) ** -0.5),
        'router_g_w': nrm((DEPTH, D_MODEL, N_GROUPS), dsc),
        'router_g_b': nrm((DEPTH, N_GROUPS), 0.01),
        'router_e_w': nrm((DEPTH, D_MODEL, N_EXPERTS), dsc),
        'router_e_b': nrm((DEPTH, N_EXPERTS), 0.01),
        'exp_w1': nrm((DEPTH, N_EXPERTS, D_MODEL, EXPERT_FF), dsc),
        'exp_w3': nrm((DEPTH, N_EXPERTS, D_MODEL, EXPERT_FF), dsc),
        'exp_w2': nrm((DEPTH, N_EXPERTS, EXPERT_FF, D_MODEL), EXPERT_FF ** -0.5),
        'final_norm': 1.0 + nrm((D_MODEL,), 0.05),
    }


def reference(x_prompt, x_sample, cache_a_k, cache_a_v, cache_a_kidx, state_b_conv, state_c_re, state_c_im,
              cache_d_k, cache_d_v, page_table, c_prompt, c_sample, norm1, norm2, ada_w, ada_b, w_in_ab, conv_w,
              w_out_ab, w_in_cd, s5_a_re, s5_a_im, s5_b_re, s5_b_im, s5_c_re, s5_c_im, s5_d, s5_log_dt, glu_w,
              glu_b, w_out_cd, router_g_w, router_g_b, router_e_w, router_e_b, exp_w1, exp_w3, exp_w2, final_norm):
    p = dict(norm1=norm1, norm2=norm2, ada_w=ada_w, ada_b=ada_b, w_in_ab=w_in_ab, conv_w=conv_w,
             w_out_ab=w_out_ab, w_in_cd=w_in_cd, s5_a_re=s5_a_re, s5_a_im=s5_a_im, s5_b_re=s5_b_re,
             s5_b_im=s5_b_im, s5_c_re=s5_c_re, s5_c_im=s5_c_im, s5_d=s5_d, s5_log_dt=s5_log_dt, glu_w=glu_w,
             glu_b=glu_b, w_out_cd=w_out_cd, router_g_w=router_g_w, router_g_b=router_g_b,
             router_e_w=router_e_w, router_e_b=router_e_b, exp_w1=exp_w1, exp_w3=exp_w3, exp_w2=exp_w2,
             final_norm=final_norm)
    past = dict(cache_a_k=cache_a_k, cache_a_v=cache_a_v, cache_a_kidx=cache_a_kidx, state_b_conv=state_b_conv,
                state_c_re=state_c_re, state_c_im=state_c_im, cache_d_k=cache_d_k, cache_d_v=cache_d_v,
                page_table=page_table)
    y_prompt, pa_k, pa_v, pa_kidx, pb_conv, pc_re, pc_im, pd_k, pd_v = trunk(x_prompt, c_prompt, p, None)
    y_sample, sa_k, sa_v, sa_kidx, sb_conv, sc_re, sc_im, sd_k, sd_v = trunk(x_sample, c_sample, p, past)
    return (y_prompt, y_sample, pa_k, pa_v, pa_kidx, pb_conv, pc_re, pc_im, pd_k, pd_v,
            sa_k, sa_v, sa_kidx, sb_conv, sc_re, sc_im, sd_k, sd_v)
```

```python
import functools
import math

import jax
import jax.numpy as jnp
import numpy as np
from jax import lax
from jax.experimental import pallas as pl
from jax.experimental.pallas import tpu as pltpu

F32 = jnp.float32
BF16 = jnp.bfloat16
I32 = jnp.int32

HEAD_DIM = 128
IDX_HEADS = 16
IDX_DIM = 64
IDX_TOPK = 256
CONV_W = 3
S5_GROUP = 16
S5_STATE = 64
MOBA_BLOCK = 256
MOBA_TOPK = 3
N_GROUPS = 4
EXP_PER_GROUP = 8
N_EXPERTS = N_GROUPS * EXP_PER_GROUP
MOE_TOPK = 2
PAGE_SIZE = 128
EPS = 1e-6

LANES = 128
NEG_BIG = -1e30
INT_MIN = -(2 ** 31)
VMEM_LIMIT = 56 * 1024 * 1024


def _params(*sem):
    return pltpu.CompilerParams(dimension_semantics=sem, vmem_limit_bytes=VMEM_LIMIT)


def _dot(a, b):
    return jnp.dot(a, b, preferred_element_type=F32)


def _dot_nt(a, b):
    return lax.dot_general(a, b, (((1,), (1,)), ((), ())), preferred_element_type=F32)


def _split(x):
    hi = x.astype(BF16)
    return hi, (x - hi.astype(F32)).astype(BF16)


def _split_stack(x):
    hi, lo = _split(x.astype(F32))
    return jnp.stack([hi, lo])


def _dot3(a, b, dot=_dot):
    ah, al = a if isinstance(a, tuple) else _split(a)
    bh, bl = b if isinstance(b, tuple) else _split(b)
    return dot(ah, bh) + dot(ah, bl) + dot(al, bh)


def _dot3_nt(a, b):
    return _dot3(a, b, _dot_nt)


def _pair(ref, *idx):
    return ref[(0,) + idx], ref[(1,) + idx]


def _ada_kernel(c_ref, w_ref, b_ref, o_ref):
    c = c_ref[...]
    o_ref[0] = _dot3(c * jax.nn.sigmoid(c), w_ref[0]) + b_ref[0]


def ada_mod(c, ada_w, ada_b):
    depth, d, n6 = ada_w.shape
    r = c.shape[0]
    tn = 1024
    return pl.pallas_call(
        _ada_kernel,
        grid=(depth, n6 // tn),
        in_specs=[pl.BlockSpec((r, d), lambda l, j: (0, 0)),
                  pl.BlockSpec((1, d, tn), lambda l, j: (l, 0, j)),
                  pl.BlockSpec((1, 1, tn), lambda l, j: (l, 0, j))],
        out_specs=pl.BlockSpec((1, r, tn), lambda l, j: (l, 0, j)),
        out_shape=jax.ShapeDtypeStruct((depth, r, n6), F32),
        compiler_params=_params("parallel", "parallel"),
    )(c, ada_w, ada_b.reshape(depth, 1, n6))


def _norm_mod(x, g, sc, sh):
    y = x * lax.rsqrt(jnp.mean(x * x, axis=-1, keepdims=True) + EPS)
    return (y * g) * (1.0 + sc) + sh


def _norm_proj_kernel(x_ref, g_ref, sc_ref, sh_ref, w_ref, o_ref, hn_ref):
    @pl.when(pl.program_id(1) == 0)
    def _():
        hi, lo = _split(_norm_mod(x_ref[...], g_ref[...], sc_ref[0, 0], sh_ref[0, 0]))
        hn_ref[0] = hi
        hn_ref[1] = lo

    o_ref[...] = _dot3(_pair(hn_ref), _pair(w_ref))


def _mod_spec(which, tm, rows_per_mod, rm, d, nargs):
    if nargs == 2:
        return pl.BlockSpec((1, 1, rm, d), lambda i, j: (which, (i * tm) // rows_per_mod, 0, 0))
    return pl.BlockSpec((1, 1, rm, d), lambda i: (which, (i * tm) // rows_per_mod, 0, 0))


def norm_proj(x, g, mod6, sc_idx, sh_idx, w, tm, rows_per_mod):
    n, d = x.shape
    nc = w.shape[2]
    tn = 512
    rm = mod6.shape[2]
    return pl.pallas_call(
        _norm_proj_kernel,
        grid=(n // tm, nc // tn),
        in_specs=[pl.BlockSpec((tm, d), lambda i, j: (i, 0)),
                  pl.BlockSpec((1, d), lambda i, j: (0, 0)),
                  _mod_spec(sc_idx, tm, rows_per_mod, rm, d, 2),
                  _mod_spec(sh_idx, tm, rows_per_mod, rm, d, 2),
                  pl.BlockSpec((2, d, tn), lambda i, j: (0, 0, j))],
        out_specs=pl.BlockSpec((tm, tn), lambda i, j: (i, j)),
        out_shape=jax.ShapeDtypeStruct((n, nc), F32),
        scratch_shapes=[pltpu.VMEM((2, tm, d), BF16)],
        compiler_params=_params("parallel", "arbitrary"),
    )(x, g.reshape(1, d), mod6, mod6, w)


def _out_proj_kernel(a_ref, b_ref, wa_ref, wb_ref, x_ref, g_ref, o_ref):
    y = _dot3(a_ref[...], _pair(wa_ref)) + _dot3(b_ref[...], _pair(wb_ref))
    o_ref[...] = x_ref[...] + g_ref[0, 0] * y


def out_proj(a, b, w, x, mod6, g_idx, tm, rows_per_mod):
    n, d = x.shape
    ka, kb = a.shape[1], b.shape[1]
    rm = mod6.shape[2]
    tn = d // 2
    wa, wb = _split_stack(w[:ka]), _split_stack(w[ka:])
    return pl.pallas_call(
        _out_proj_kernel,
        grid=(n // tm, d // tn),
        in_specs=[pl.BlockSpec((tm, ka), lambda i, j: (i, 0)),
                  pl.BlockSpec((tm, kb), lambda i, j: (i, 0)),
                  pl.BlockSpec((2, ka, tn), lambda i, j: (0, 0, j)),
                  pl.BlockSpec((2, kb, tn), lambda i, j: (0, 0, j)),
                  pl.BlockSpec((tm, tn), lambda i, j: (i, j)),
                  pl.BlockSpec((1, 1, rm, tn), lambda i, j: (g_idx, (i * tm) // rows_per_mod, 0, j))],
        out_specs=pl.BlockSpec((tm, tn), lambda i, j: (i, j)),
        out_shape=jax.ShapeDtypeStruct((n, d), F32),
        compiler_params=_params("parallel", "parallel"),
    )(a, b, wa, wb, x, mod6)


def _final_norm_kernel(x_ref, g_ref, o_ref):
    x = x_ref[...]
    o_ref[...] = (x * lax.rsqrt(jnp.mean(x * x, axis=-1, keepdims=True) + EPS)) * g_ref[...]


def final_norm(x, g, tm):
    n, d = x.shape
    return pl.pallas_call(
        _final_norm_kernel,
        grid=(n // tm,),
        in_specs=[pl.BlockSpec((tm, d), lambda i: (i, 0)), pl.BlockSpec((1, d), lambda i: (0, 0))],
        out_specs=pl.BlockSpec((tm, d), lambda i: (i, 0)),
        out_shape=jax.ShapeDtypeStruct((n, d), F32),
        compiler_params=_params("parallel"),
    )(x, g.reshape(1, d))


def _conv_prompt_kernel(gb_ref, gc_ref, gh_ref, pc_ref, ph_ref, w_ref, o_ref, nb_ref):
    j = pl.program_id(1)
    tq = gc_ref.shape[0]
    u = gc_ref[...] * gh_ref[...]
    prev = pc_ref[...] * ph_ref[...]
    prev = jnp.where(j == 0, 0.0, prev)
    row = lax.broadcasted_iota(I32, u.shape, 0)
    u1 = jnp.where(row == 0, prev[7:8], pltpu.roll(u, 1, 0))
    u2 = jnp.where(row == 0, prev[6:7], jnp.where(row == 1, prev[7:8], pltpu.roll(u, 2, 0)))
    y = w_ref[0:1] * u2 + w_ref[1:2] * u1 + w_ref[2:3] * u
    o_ref[...] = (gb_ref[...] * y).astype(o_ref.dtype)
    nb_ref[0] = u[tq - 8:tq]


def conv_prompt(proj, conv_w, bsz, t, tq):
    n = proj.shape[0]
    w = conv_w.shape[1]
    nt = t // tq
    wpad = jnp.concatenate([conv_w, jnp.zeros((8 - CONV_W, w), F32)], axis=0)

    def halo(cb):
        return pl.BlockSpec((8, w), lambda b, j: (jnp.maximum((b * t + j * tq) // 8 - 1, 0), cb))

    return pl.pallas_call(
        _conv_prompt_kernel,
        grid=(bsz, nt),
        in_specs=[pl.BlockSpec((tq, w), lambda b, j: (b * nt + j, 4)),
                  pl.BlockSpec((tq, w), lambda b, j: (b * nt + j, 5)),
                  pl.BlockSpec((tq, w), lambda b, j: (b * nt + j, 6)),
                  halo(5), halo(6),
                  pl.BlockSpec((8, w), lambda b, j: (0, 0))],
        out_specs=[pl.BlockSpec((tq, w), lambda b, j: (b * nt + j, 0)),
                   pl.BlockSpec((1, 8, w), lambda b, j: (b, 0, 0))],
        out_shape=[jax.ShapeDtypeStruct((n, w), F32), jax.ShapeDtypeStruct((bsz, 8, w), F32)],
        compiler_params=_params("parallel", "arbitrary"),
    )(proj, proj, proj, proj, proj, wpad)


def _conv_sample_kernel(t_len, gb_ref, gc_ref, gh_ref, p0_ref, p1_ref, w_ref, o_ref, u_ref):
    u = gc_ref[...] * gh_ref[...]
    row = lax.broadcasted_iota(I32, u.shape, 0) % t_len
    u1 = jnp.where(row == 0, p1_ref[...], pltpu.roll(u, 1, 0))
    u2 = jnp.where(row == 0, p0_ref[...], jnp.where(row == 1, p1_ref[...], pltpu.roll(u, 2, 0)))
    y = w_ref[0:1] * u2 + w_ref[1:2] * u1 + w_ref[2:3] * u
    o_ref[...] = (gb_ref[...] * y).astype(o_ref.dtype)
    u_ref[...] = u


def conv_sample(proj, conv_w, buf, t_len):
    n = proj.shape[0]
    w = conv_w.shape[1]
    wpad = jnp.concatenate([conv_w, jnp.zeros((8 - CONV_W, w), F32)], axis=0)
    p0 = jnp.repeat(buf[:, 0], t_len, axis=0)
    p1 = jnp.repeat(buf[:, 1], t_len, axis=0)
    full = lambda cb: pl.BlockSpec((n, w), lambda i: (0, cb))
    return pl.pallas_call(
        functools.partial(_conv_sample_kernel, t_len),
        grid=(1,),
        in_specs=[full(4), full(5), full(6), full(0), full(0), pl.BlockSpec((8, w), lambda i: (0, 0))],
        out_specs=[full(0), full(0)],
        out_shape=[jax.ShapeDtypeStruct((n, w), F32), jax.ShapeDtypeStruct((n, w), F32)],
        compiler_params=_params("arbitrary"),
    )(proj, proj, proj, p0, p1, wpad)


def _float_key(s):
    b = pltpu.bitcast(s, I32)
    return b ^ (lax.shift_right_arithmetic(b, 31) & 0x7FFFFFFF)


def _dsa_prompt_kernel(n_sel, tk, q_ref, iq_ref, iw_ref, ike_ref, iko_ref, k_ref, v_ref, o_ref, key_ref, bias_ref):
    i = pl.program_id(1)
    tq = q_ref.shape[0]
    nck = ((i + 1) * tq + tk - 1) // tk

    @pl.when(pl.program_id(2) == 0)
    def _():
        _dsa_prompt_select(n_sel, tk, nck, i, iq_ref, iw_ref, ike_ref, iko_ref, key_ref, bias_ref)

    qh = _split(q_ref[...])

    def att_chunk(c, carry):
        m, l, acc = carry
        off = pl.multiple_of(c * tk, tk)
        s = _dot3_nt(qh, k_ref[pl.ds(off, tk), :]) * HEAD_DIM ** -0.5
        s = jnp.where(bias_ref[:, pl.ds(off, tk)] == 0.0, s, NEG_BIG)
        m_new = jnp.maximum(m, jnp.max(s, axis=-1, keepdims=True))
        alpha = jnp.exp(m - m_new)
        p = jnp.exp(s - m_new)
        l = alpha * l + jnp.sum(p, axis=-1, keepdims=True)
        acc = alpha * acc + _dot3(p, v_ref[pl.ds(off, tk), :])
        return m_new, l, acc

    m, l, acc = lax.fori_loop(
        0, nck, att_chunk,
        (jnp.full((tq, 1), NEG_BIG, F32), jnp.zeros((tq, 1), F32), jnp.zeros((tq, HEAD_DIM), F32)))
    o_ref[...] = acc / l


def _dsa_prompt_select(n_sel, tk, nck, i, iq_ref, iw_ref, ike_ref, iko_ref, key_ref, bias_ref):
    tq = iq_ref.shape[0]
    row = i * tq + lax.broadcasted_iota(I32, (tq, tk), 0)
    col0 = lax.broadcasted_iota(I32, (tq, tk), 1)
    iw = iw_ref[...]

    def score_chunk(c, carry):
        off = pl.multiple_of(c * tk, tk)
        ke = _pair(ike_ref, pl.ds(off, tk))
        ko = _pair(iko_ref, pl.ds(off, tk))
        acc = jnp.zeros((tq, tk), F32)
        for hp in range(IDX_HEADS // 2):
            iq2 = _split(iq_ref[:, hp * LANES:(hp + 1) * LANES])
            s0 = _dot3_nt(iq2, ke) * IDX_DIM ** -0.5
            s1 = _dot3_nt(iq2, ko) * IDX_DIM ** -0.5
            acc = acc + jnp.maximum(s0, 0.0) * iw[:, IDX_DIM + 2 * hp:IDX_DIM + 2 * hp + 1]
            acc = acc + jnp.maximum(s1, 0.0) * iw[:, IDX_DIM + 2 * hp + 1:IDX_DIM + 2 * hp + 2]
        acc = acc * IDX_HEADS ** -0.5
        acc = jnp.where(off + col0 <= row, acc, -jnp.inf)
        key_ref[:, pl.ds(off, tk)] = _float_key(acc)
        return carry

    lax.fori_loop(0, nck, score_chunk, 0)

    def count_ge(cand):
        def body(c, acc):
            off = pl.multiple_of(c * tk, tk)
            m = jnp.where(key_ref[:, pl.ds(off, tk)] >= cand, 1.0, 0.0)
            for q in range(tk // LANES):
                acc = acc + m[:, q * LANES:(q + 1) * LANES]
            return acc
        acc = lax.fori_loop(0, nck, body, jnp.zeros((tq, LANES), F32))
        return jnp.sum(acc, axis=-1, keepdims=True)

    def bit_step(t, cur):
        cand = cur + lax.shift_left(jnp.int32(1), 31 - t)
        return jnp.where(count_ge(cand) >= n_sel, cand, cur)

    thr = lax.fori_loop(0, 32, bit_step, jnp.full((tq, 1), INT_MIN, I32))

    def bias_chunk(c, carry):
        off = pl.multiple_of(c * tk, tk)
        sel = (key_ref[:, pl.ds(off, tk)] >= thr) & (off + col0 <= row)
        bias_ref[:, pl.ds(off, tk)] = jnp.where(sel, 0.0, NEG_BIG)
        return carry

    lax.fori_loop(0, nck, bias_chunk, 0)


def dsa_prompt(proj, ik_e, ik_o, n_heads, bsz, t, tq, tk):
    n = proj.shape[0]
    nt = t // tq
    n_sel = min(IDX_TOPK, t // 4)
    return pl.pallas_call(
        functools.partial(_dsa_prompt_kernel, n_sel, tk),
        grid=(bsz, nt, n_heads),
        in_specs=[pl.BlockSpec((tq, HEAD_DIM), lambda b, i, h: (b * nt + i, h)),
                  pl.BlockSpec((tq, IDX_HEADS * IDX_DIM), lambda b, i, h: (b * nt + i, 3)),
                  pl.BlockSpec((tq, LANES), lambda b, i, h: (b * nt + i, 7 * n_heads)),
                  pl.BlockSpec((2, t, LANES), lambda b, i, h: (0, b, 0)),
                  pl.BlockSpec((2, t, LANES), lambda b, i, h: (0, b, 0)),
                  pl.BlockSpec((t, HEAD_DIM), lambda b, i, h: (b, n_heads + h)),
                  pl.BlockSpec((t, HEAD_DIM), lambda b, i, h: (b, 2 * n_heads + h))],
        out_specs=pl.BlockSpec((tq, HEAD_DIM), lambda b, i, h: (b * nt + i, h)),
        out_shape=jax.ShapeDtypeStruct((n, n_heads * HEAD_DIM), F32),
        scratch_shapes=[pltpu.VMEM((tq, t), I32), pltpu.VMEM((tq, t), F32)],
        compiler_params=_params("parallel", "arbitrary", "arbitrary"),
    )(proj, proj, proj, ik_e, ik_o, proj, proj)


def s5_weights(log_dt, a_re, a_im, b_re, b_im, c_re, c_im, chunk):
    hp = lax.Precision.HIGHEST
    g, p = a_re.shape
    dt = jnp.exp(log_dt.astype(F32))[:, None]
    lr, li = a_re.astype(F32), a_im.astype(F32)
    mag = jnp.exp(lr * dt)
    ab_re, ab_im = mag * jnp.cos(li * dt), mag * jnp.sin(li * dt)
    den = lr * lr + li * li
    f_re = ((ab_re - 1.0) * lr + ab_im * li) / den
    f_im = (ab_im * lr - (ab_re - 1.0) * li) / den
    br, bi = b_re.astype(F32), b_im.astype(F32)
    bb_re = f_re[..., None] * br - f_im[..., None] * bi
    bb_im = f_re[..., None] * bi + f_im[..., None] * br
    pr, pi = [jnp.ones_like(ab_re)], [jnp.zeros_like(ab_im)]
    for _ in range(chunk):
        pr.append(pr[-1] * ab_re - pi[-1] * ab_im)
        pi.append(pr[-2] * ab_im + pi[-1] * ab_re)
    pw_re, pw_im = jnp.stack(pr), jnp.stack(pi)
    cr, ci = c_re.astype(F32), c_im.astype(F32)
    rev_re = jnp.stack(pr[chunk - 1::-1])
    rev_im = jnp.stack(pi[chunk - 1::-1])
    w_re = rev_re[..., None] * bb_re[None] - rev_im[..., None] * bb_im[None]
    w_im = rev_re[..., None] * bb_im[None] + rev_im[..., None] * bb_re[None]
    z_re = cr[None] * pw_re[1:, :, None, :] - ci[None] * pw_im[1:, :, None, :]
    z_im = cr[None] * pw_im[1:, :, None, :] + ci[None] * pw_re[1:, :, None, :]
    ab_b_re = pw_re[:chunk, :, :, None] * bb_re[None] - pw_im[:chunk, :, :, None] * bb_im[None]
    ab_b_im = pw_re[:chunk, :, :, None] * bb_im[None] + pw_im[:chunk, :, :, None] * bb_re[None]
    k_tap = (jnp.einsum('gop,tgpi->tgoi', cr, ab_b_re, precision=hp)
             - jnp.einsum('gop,tgpi->tgoi', ci, ab_b_im, precision=hp))

    nblk = g * S5_GROUP // LANES
    gpb = g // nblk
    eye = jnp.eye(gpb, dtype=F32)

    def blockdiag(m, rows_last):
        c_, _, r_, k_ = m.shape
        m5 = m.reshape(c_, nblk, gpb, r_, k_)
        out = jnp.einsum('tbgrk,gh->btgrhk', m5, eye)
        return out.reshape(nblk, c_, gpb * r_, gpb * k_)

    w_re_b = _split_stack(blockdiag(jnp.swapaxes(w_re, 2, 3), None))
    w_im_b = _split_stack(blockdiag(jnp.swapaxes(w_im, 2, 3), None))
    v_re_b = _split_stack(blockdiag(jnp.swapaxes(z_re, 2, 3), None))
    v_im_b = _split_stack(blockdiag(jnp.swapaxes(-z_im, 2, 3), None))
    k_b = _split_stack(blockdiag(jnp.swapaxes(k_tap, 2, 3), None))
    al_re = pw_re[chunk].reshape(nblk, 1, gpb * p)
    al_im = pw_im[chunk].reshape(nblk, 1, gpb * p)
    return w_re_b, w_im_b, v_re_b, v_im_b, k_b, al_re, al_im


def _s5_kernel(chunk, scan, u_ref, h0r_ref, h0i_ref, wre_ref, wim_ref, vre_ref, vim_ref, k_ref, ar_ref, ai_ref,
               d_ref, y_ref, hr_out, hi_out, hr_s, hi_s, sr_s, si_s):
    rc = sr_s.shape[0]
    xs = [u_ref[pl.ds(tok, rc, stride=chunk), :] for tok in range(chunk)]
    xb = [_split(x) for x in xs]
    sre = _dot3(xb[0], _pair(wre_ref, 0, 0))
    sim = _dot3(xb[0], _pair(wim_ref, 0, 0))
    for tok in range(1, chunk):
        sre = sre + _dot3(xb[tok], _pair(wre_ref, 0, tok))
        sim = sim + _dot3(xb[tok], _pair(wim_ref, 0, tok))
    ar, ai = ar_ref[0], ai_ref[0]
    if scan:
        @pl.when(pl.program_id(2) == 0)
        def _():
            hr_s[...] = h0r_ref[0, 0]
            hi_s[...] = h0i_ref[0, 0]

        sr_s[...] = sre
        si_s[...] = sim

        def body(c, carry):
            hr, hi = carry
            sr = sr_s[pl.ds(c, 1), :]
            si = si_s[pl.ds(c, 1), :]
            sr_s[pl.ds(c, 1), :] = hr
            si_s[pl.ds(c, 1), :] = hi
            return ar * hr - ai * hi + sr, ar * hi + ai * hr + si

        hr, hi = lax.fori_loop(0, rc, body, (hr_s[...], hi_s[...]))
        hr_s[...] = hr
        hi_s[...] = hi
        hr_out[0, 0] = hr
        hi_out[0, 0] = hi
        h_re, h_im = sr_s[...], si_s[...]
    else:
        h_re, h_im = h0r_ref[0], h0i_ref[0]
        hr_out[0] = ar * h_re - ai * h_im + sre
        hi_out[0] = ar * h_im + ai * h_re + sim
    hb_re, hb_im = _split(h_re), _split(h_im)
    for tok in range(chunk):
        acc = _dot3(hb_re, _pair(vre_ref, 0, tok)) + _dot3(hb_im, _pair(vim_ref, 0, tok))
        for s in range(tok + 1):
            acc = acc + _dot3(xb[s], _pair(k_ref, 0, tok - s))
        acc = acc + d_ref[0] * xs[tok]
        y_ref[pl.ds(tok, rc, stride=chunk), :] = acc


def s5_prompt(proj, h0r, h0i, wts, d_skip, bsz, t, chunk, rc):
    w_re, w_im, v_re, v_im, k_b, al_re, al_im = wts
    n = proj.shape[0]
    nblk = w_re.shape[1]
    sw = w_re.shape[4]
    cw = nblk * LANES
    nt = t // (rc * chunk)
    wspec = lambda a: pl.BlockSpec((2, 1) + a.shape[2:], lambda k, b, j: (0, k) + (0,) * (a.ndim - 2))
    aspec = lambda a: pl.BlockSpec((1,) + a.shape[1:], lambda k, b, j: (k,) + (0,) * (a.ndim - 1))
    hspec = pl.BlockSpec((1, 1, 1, sw), lambda k, b, j: (k, b, 0, 0))
    return pl.pallas_call(
        functools.partial(_s5_kernel, chunk, True),
        grid=(nblk, bsz, nt),
        in_specs=[pl.BlockSpec((rc * chunk, LANES), lambda k, b, j: (b * nt + j, k)),
                  hspec, hspec,
                  wspec(w_re), wspec(w_im), wspec(v_re), wspec(v_im), wspec(k_b), aspec(al_re), aspec(al_im),
                  pl.BlockSpec((1, 1, LANES), lambda k, b, j: (k, 0, 0))],
        out_specs=[pl.BlockSpec((rc * chunk, LANES), lambda k, b, j: (b * nt + j, k)), hspec, hspec],
        out_shape=[jax.ShapeDtypeStruct((n, cw), F32),
                   jax.ShapeDtypeStruct((nblk, bsz, 1, sw), F32),
                   jax.ShapeDtypeStruct((nblk, bsz, 1, sw), F32)],
        scratch_shapes=[pltpu.VMEM((1, sw), F32), pltpu.VMEM((1, sw), F32),
                        pltpu.VMEM((rc, sw), F32), pltpu.VMEM((rc, sw), F32)],
        compiler_params=_params("parallel", "parallel", "arbitrary"),
    )(proj, h0r, h0i, w_re, w_im, v_re, v_im, k_b, al_re, al_im, d_skip.reshape(nblk, 1, LANES))


def s5_sample(proj, h0r, h0i, wts, d_skip, bsz, chunk):
    w_re, w_im, v_re, v_im, k_b, al_re, al_im = wts
    n = proj.shape[0]
    nblk = w_re.shape[1]
    sw = w_re.shape[4]
    cw = nblk * LANES
    wspec = lambda a: pl.BlockSpec((2, 1) + a.shape[2:], lambda k: (0, k) + (0,) * (a.ndim - 2))
    aspec = lambda a: pl.BlockSpec((1,) + a.shape[1:], lambda k: (k,) + (0,) * (a.ndim - 1))
    hspec = pl.BlockSpec((1, bsz, sw), lambda k: (k, 0, 0))
    return pl.pallas_call(
        functools.partial(_s5_kernel, chunk, False),
        grid=(nblk,),
        in_specs=[pl.BlockSpec((n, LANES), lambda k: (0, k)),
                  hspec, hspec,
                  wspec(w_re), wspec(w_im), wspec(v_re), wspec(v_im), wspec(k_b), aspec(al_re), aspec(al_im),
                  pl.BlockSpec((1, 1, LANES), lambda k: (k, 0, 0))],
        out_specs=[pl.BlockSpec((n, LANES), lambda k: (0, k)), hspec, hspec],
        out_shape=[jax.ShapeDtypeStruct((n, cw), F32),
                   jax.ShapeDtypeStruct((nblk, bsz, sw), F32),
                   jax.ShapeDtypeStruct((nblk, bsz, sw), F32)],
        scratch_shapes=[pltpu.VMEM((1, sw), F32), pltpu.VMEM((1, sw), F32),
                        pltpu.VMEM((bsz, sw), F32), pltpu.VMEM((bsz, sw), F32)],
        compiler_params=_params("parallel"),
    )(proj, h0r, h0i, w_re, w_im, v_re, v_im, k_b, al_re, al_im, d_skip.reshape(nblk, 1, LANES))


def _glu_kernel(y_ref, w_ref, b_ref, o_ref):
    z = jax.nn.gelu(y_ref[...], approximate=True)
    o_ref[...] = z * jax.nn.sigmoid(_dot3(z, _pair(w_ref)) + b_ref[...])


def glu(y, w, b, tm):
    n, cw = y.shape
    return pl.pallas_call(
        _glu_kernel,
        grid=(n // tm,),
        in_specs=[pl.BlockSpec((tm, cw), lambda i: (i, 0)),
                  pl.BlockSpec((2, cw, cw), lambda i: (0, 0, 0)),
                  pl.BlockSpec((1, cw), lambda i: (0, 0))],
        out_specs=pl.BlockSpec((tm, cw), lambda i: (i, 0)),
        out_shape=jax.ShapeDtypeStruct((n, cw), F32),
        compiler_params=_params("parallel"),
    )(y, _split_stack(w), b.reshape(1, cw))


def _moba_prompt_kernel(nb, q_ref, k_ref, v_ref, o_ref, kmean_ref):
    i = pl.program_id(2)
    tq = q_ref.shape[0]

    @pl.when(i == 0)
    def _():
        kmean_ref[...] = jnp.zeros_like(kmean_ref)
        for n in range(nb):
            kmean_ref[n:n + 1, :] = jnp.mean(k_ref[n * MOBA_BLOCK:(n + 1) * MOBA_BLOCK, :], axis=0, keepdims=True)

    qb = _split(q_ref[...])
    gate = _dot3_nt(qb, kmean_ref[...])
    blk = lax.broadcasted_iota(I32, gate.shape, 1)
    gate = jnp.where(blk < i, gate, -jnp.inf)
    sel = jnp.zeros(gate.shape, F32)
    for _ in range(MOBA_TOPK):
        top = jnp.max(gate, axis=-1, keepdims=True)
        first = jnp.min(jnp.where(gate == top, blk, nb + LANES), axis=-1, keepdims=True)
        pick = (blk == first) & (first < i)
        sel = jnp.where(pick, 1.0, sel)
        gate = jnp.where(blk == first, -jnp.inf, gate)

    row = lax.broadcasted_iota(I32, (tq, MOBA_BLOCK), 0)
    col = lax.broadcasted_iota(I32, (tq, MOBA_BLOCK), 1)

    def att_block(n, carry):
        m, l, acc = carry
        off = pl.multiple_of(n * MOBA_BLOCK, MOBA_BLOCK)
        s = _dot3_nt(qb, k_ref[pl.ds(off, MOBA_BLOCK), :]) * HEAD_DIM ** -0.5
        picked = jnp.max(jnp.where(blk == n, sel, 0.0), axis=-1, keepdims=True)
        valid = jnp.where(n == i, jnp.where(col <= row, 1.0, 0.0), picked) > 0.0
        s = jnp.where(valid, s, NEG_BIG)
        m_new = jnp.maximum(m, jnp.max(s, axis=-1, keepdims=True))
        alpha = jnp.exp(m - m_new)
        p = jnp.where(valid, jnp.exp(s - m_new), 0.0)
        l = alpha * l + jnp.sum(p, axis=-1, keepdims=True)
        acc = alpha * acc + _dot3(p, v_ref[pl.ds(off, MOBA_BLOCK), :])
        return m_new, l, acc

    m, l, acc = lax.fori_loop(
        0, i + 1, att_block,
        (jnp.full((tq, 1), NEG_BIG, F32), jnp.zeros((tq, 1), F32), jnp.zeros((tq, HEAD_DIM), F32)))
    o_ref[...] = acc / l


def moba_prompt(proj, bsz, t):
    n = proj.shape[0]
    n_heads = proj.shape[1] // 4 // HEAD_DIM
    nb = t // MOBA_BLOCK
    nbp = -(-nb // 8) * 8
    return pl.pallas_call(
        functools.partial(_moba_prompt_kernel, nb),
        grid=(bsz, n_heads, nb),
        in_specs=[pl.BlockSpec((MOBA_BLOCK, HEAD_DIM), lambda b, h, i: (b * nb + i, n_heads + h)),
                  pl.BlockSpec((t, HEAD_DIM), lambda b, h, i: (b, 2 * n_heads + h)),
                  pl.BlockSpec((t, HEAD_DIM), lambda b, h, i: (b, 3 * n_heads + h))],
        out_specs=pl.BlockSpec((MOBA_BLOCK, HEAD_DIM), lambda b, h, i: (b * nb + i, h)),
        out_shape=jax.ShapeDtypeStruct((n, n_heads * HEAD_DIM), F32),
        scratch_shapes=[pltpu.VMEM((nbp, HEAD_DIM), F32)],
        compiler_params=_params("parallel", "parallel", "arbitrary"),
    )(proj, proj, proj)


def _router_kernel(x_ref, g_ref, sc_ref, sh_ref, w_ref, b_ref, hn_ref, eid_ref, rank_ref, wt_ref, cnt_ref, carry_ref):
    i = pl.program_id(0)
    tm = x_ref.shape[0]

    @pl.when(i == 0)
    def _():
        carry_ref[...] = jnp.zeros_like(carry_ref)

    hn = _norm_mod(x_ref[...], g_ref[...], sc_ref[0, 0], sh_ref[0, 0])
    hn_ref[...] = hn
    logit = _dot3(hn, w_ref[...]) + b_ref[...]
    lane = lax.broadcasted_iota(I32, logit.shape, 1)
    gmask = lane < N_GROUPS
    gl = jnp.where(gmask, logit, -jnp.inf)
    gmax = jnp.max(gl, axis=-1, keepdims=True)
    gsel = jnp.min(jnp.where(gl == gmax, lane, LANES), axis=-1, keepdims=True)
    g_gate = 1.0 / jnp.sum(jnp.where(gmask, jnp.exp(gl - gmax), 0.0), axis=-1, keepdims=True)
    lo = N_GROUPS + gsel * EXP_PER_GROUP
    emask = (lane >= lo) & (lane < lo + EXP_PER_GROUP)
    el = jnp.where(emask, logit, -jnp.inf)
    v1 = jnp.max(el, axis=-1, keepdims=True)
    i1 = jnp.min(jnp.where(el == v1, lane, LANES), axis=-1, keepdims=True)
    el2 = jnp.where(lane == i1, -jnp.inf, el)
    v2 = jnp.max(el2, axis=-1, keepdims=True)
    i2 = jnp.min(jnp.where(el2 == v2, lane, LANES), axis=-1, keepdims=True)
    e21 = jnp.exp(v2 - v1)
    w1 = g_gate / (1.0 + e21)
    w2 = w1 * e21
    oh1 = jnp.where(lane == i1, 1.0, 0.0)
    oh2 = jnp.where(lane == i2, 1.0, 0.0)
    oh = oh1 + oh2
    r_i = lax.broadcasted_iota(I32, (tm, tm), 0)
    c_i = lax.broadcasted_iota(I32, (tm, tm), 1)
    tri = jnp.where(c_i < r_i, 1.0, 0.0).astype(BF16)
    before = _dot(tri, oh.astype(BF16)) + carry_ref[0:1, :]
    rank1 = jnp.sum(oh1 * before, axis=-1, keepdims=True).astype(I32)
    rank2 = jnp.sum(oh2 * before, axis=-1, keepdims=True).astype(I32)
    eid_ref[...] = jnp.where(lane == 0, i1 - N_GROUPS, jnp.where(lane == 1, i2 - N_GROUPS, 0))
    rank_ref[...] = jnp.where(lane == 0, rank1, jnp.where(lane == 1, rank2, 0))
    wt_ref[...] = jnp.where(lane == 0, w1, jnp.where(lane == 1, w2, 0.0))
    carry_ref[0:1, :] = carry_ref[0:1, :] + jnp.sum(oh, axis=0, keepdims=True)
    cnt_ref[...] = carry_ref[...]


def router(x, g, mod6, w_r, b_r, tm, rows_per_mod):
    n, d = x.shape
    rm = mod6.shape[2]
    tok = lambda width, dt: (pl.BlockSpec((tm, width), lambda i: (i, 0)), jax.ShapeDtypeStruct((n, width), dt))
    specs = [tok(d, F32), tok(LANES, I32), tok(LANES, I32), tok(LANES, F32),
             (pl.BlockSpec((8, LANES), lambda i: (0, 0)), jax.ShapeDtypeStruct((8, LANES), F32))]
    return pl.pallas_call(
        _router_kernel,
        grid=(n // tm,),
        in_specs=[pl.BlockSpec((tm, d), lambda i: (i, 0)),
                  pl.BlockSpec((1, d), lambda i: (0, 0)),
                  _mod_spec(4, tm, rows_per_mod, rm, d, 1),
                  _mod_spec(3, tm, rows_per_mod, rm, d, 1),
                  pl.BlockSpec((d, LANES), lambda i: (0, 0)),
                  pl.BlockSpec((1, LANES), lambda i: (0, 0))],
        out_specs=[s for s, _ in specs],
        out_shape=[o for _, o in specs],
        scratch_shapes=[pltpu.VMEM((8, LANES), F32)],
        compiler_params=_params("arbitrary"),
    )(x, g.reshape(1, d), mod6, mod6, w_r, b_r)


def _expert_kernel(te_ref, tv_ref, src_ref, dst_ref, hn_hbm, w1_ref, w3_ref, w2_ref, y_hbm, xbuf, ybuf, gsem, ssem):
    j = pl.program_id(0)
    tm = xbuf.shape[0]

    @pl.when(tv_ref[j] > 0)
    def _():
        base = j * tm

        def gather(r):
            return pltpu.make_async_copy(hn_hbm.at[pl.ds(src_ref[base + r], 1)], xbuf.at[pl.ds(r, 1)], gsem)

        def scatter(r):
            return pltpu.make_async_copy(ybuf.at[pl.ds(r, 1)], y_hbm.at[pl.ds(dst_ref[base + r], 1)], ssem)

        def start_gather(r, c):
            gather(r).start()
            return c

        def wait_gather(r, c):
            gather(r).wait()
            return c

        lax.fori_loop(0, tm, start_gather, 0)
        lax.fori_loop(0, tm, wait_gather, 0)
        xb = _split(xbuf[...])
        a = _dot3(xb, w1_ref[0, 0])
        b = _dot3(xb, w3_ref[0, 0])
        h = (a * jax.nn.sigmoid(a)) * b
        ybuf[...] = _dot3(h, w2_ref[0, 0])

        def start_scatter(r, c):
            @pl.when(dst_ref[base + r] >= 0)
            def _():
                scatter(r).start()
            return c

        def wait_scatter(r, c):
            @pl.when(dst_ref[base + r] >= 0)
            def _():
                scatter(r).wait()
            return c

        lax.fori_loop(0, tm, start_scatter, 0)
        lax.fori_loop(0, tm, wait_scatter, 0)


def experts(hn, tile_e, tile_v, src, dst, w1, w3, w2, layer, tm, n_out):
    n_tiles = tile_e.shape[0]
    d = hn.shape[1]
    ff = w1.shape[3]
    grid_spec = pltpu.PrefetchScalarGridSpec(
        num_scalar_prefetch=4,
        grid=(n_tiles,),
        in_specs=[pl.BlockSpec(memory_space=pl.ANY),
                  pl.BlockSpec((1, 1, d, ff), lambda j, te, tv, s, t: (layer, te[j], 0, 0)),
                  pl.BlockSpec((1, 1, d, ff), lambda j, te, tv, s, t: (layer, te[j], 0, 0)),
                  pl.BlockSpec((1, 1, ff, d), lambda j, te, tv, s, t: (layer, te[j], 0, 0))],
        out_specs=pl.BlockSpec(memory_space=pl.ANY),
        scratch_shapes=[pltpu.VMEM((tm, d), F32), pltpu.VMEM((tm, d), F32),
                        pltpu.SemaphoreType.DMA(()), pltpu.SemaphoreType.DMA(())],
    )
    return pl.pallas_call(
        _expert_kernel,
        grid_spec=grid_spec,
        out_shape=jax.ShapeDtypeStruct((n_out, d), F32),
        compiler_params=_params("arbitrary"),
    )(tile_e, tile_v, src, dst, hn, w1, w3, w2)


def _combine_kernel(x_ref, g_ref, wt_ref, y0_ref, y1_ref, o_ref):
    wt = wt_ref[...]
    o_ref[...] = x_ref[...] + g_ref[0, 0] * (wt[:, 0:1] * y0_ref[...] + wt[:, 1:2] * y1_ref[...])


def combine(x, mod6, wt, y, tm, rows_per_mod):
    n, d = x.shape
    rm = mod6.shape[2]
    nt = n // tm
    return pl.pallas_call(
        _combine_kernel,
        grid=(nt,),
        in_specs=[pl.BlockSpec((tm, d), lambda i: (i, 0)),
                  _mod_spec(5, tm, rows_per_mod, rm, d, 1),
                  pl.BlockSpec((tm, LANES), lambda i: (i, 0)),
                  pl.BlockSpec((tm, d), lambda i: (i, 0)),
                  pl.BlockSpec((tm, d), lambda i: (i + nt, 0))],
        out_specs=pl.BlockSpec((tm, d), lambda i: (i, 0)),
        out_shape=jax.ShapeDtypeStruct((n, d), F32),
        compiler_params=_params("parallel"),
    )(x, mod6, wt, y, y)


def hier_moe(x, g, mod6, w_r, b_r, w1, w3, w2, layer, tm, rows_per_mod, tm_e):
    n, d = x.shape
    hn, eid, rank, wt, cnt = router(x, g, mod6, w_r, b_r, tm, rows_per_mod)
    cnt = cnt[0, N_GROUPS:N_GROUPS + N_EXPERTS].astype(I32)
    padded = ((cnt + tm_e - 1) // tm_e) * tm_e
    ends = jnp.cumsum(padded)
    offs = ends - padded
    e2, r2 = eid[:, :MOE_TOPK], rank[:, :MOE_TOPK]
    pos = (offs[e2] + r2).reshape(-1)
    n_tiles = (n * MOE_TOPK) // tm_e + N_EXPERTS
    n_rows = n_tiles * tm_e
    tok_id = jnp.repeat(jnp.arange(n, dtype=I32), MOE_TOPK)
    pick_id = jnp.tile(jnp.arange(MOE_TOPK, dtype=I32), n)
    src = jnp.zeros((n_rows,), I32).at[pos].set(tok_id)
    dst = jnp.full((n_rows,), -1, I32).at[pos].set(pick_id * n + tok_id)
    starts = jnp.arange(n_tiles, dtype=I32) * tm_e
    tile_e = jnp.minimum(jnp.searchsorted(ends, starts, side='right'), N_EXPERTS - 1).astype(I32)
    tile_v = (starts < ends[-1]).astype(I32)
    n_out = n * MOE_TOPK
    y = experts(hn, tile_e, tile_v, src, dst, w1, w3, w2, layer, tm_e, n_out)
    return combine(x, mod6, wt, y, tm, rows_per_mod)


SAMPLE_ROWS = 8


def _dsa_select_kernel(layer, n_pages, n_sel, t_new, pt_ref, iq_ref, iw_ref, ikn_ref, kidx_hbm,
                       idx_ref, cnt_ref, mnew_ref, kbuf, score_ref, key_ref, pfx_ref, sem):
    b = pl.program_id(0)
    n_past = n_pages * PAGE_SIZE
    tp = SAMPLE_ROWS
    ck = min(2048, n_past)

    def page_copy(pg):
        return pltpu.make_async_copy(kidx_hbm.at[layer, pt_ref[b * n_pages + pg]],
                                     kbuf.at[pl.ds(pg * PAGE_SIZE, PAGE_SIZE)], sem)

    def start(pg, c):
        page_copy(pg).start()
        return c

    def wait(pg, c):
        page_copy(pg).wait()
        return c

    lax.fori_loop(0, n_pages, start, 0)
    lax.fori_loop(0, n_pages, wait, 0)

    iq = _split(iq_ref[0])
    iw = iw_ref[0]

    def head_sum(s):
        s = jnp.maximum(s * IDX_DIM ** -0.5, 0.0) * iw
        rows = [jnp.sum(s[t * IDX_HEADS:(t + 1) * IDX_HEADS], axis=0, keepdims=True) for t in range(tp)]
        return jnp.concatenate(rows, axis=0) * IDX_HEADS ** -0.5

    def score_chunk(c, carry):
        off = pl.multiple_of(c * ck, ck)
        sc = head_sum(_dot3_nt(iq, kbuf[pl.ds(off, ck), :]))
        score_ref[:, pl.ds(off, ck)] = sc
        key_ref[:, pl.ds(off, ck)] = _float_key(sc)
        return carry

    lax.fori_loop(0, n_past // ck, score_chunk, 0)
    ikn = jnp.concatenate([ikn_ref[0], jnp.zeros((LANES - tp, IDX_DIM), F32)], axis=0)
    s_new = head_sum(_dot3_nt(iq, ikn))
    trow = lax.broadcasted_iota(I32, (tp, LANES), 0)
    jcol = lax.broadcasted_iota(I32, (tp, LANES), 1)
    s_new = jnp.where((jcol <= trow) & (jcol < t_new), s_new, -jnp.inf)
    key_new = _float_key(s_new)

    def count_ge(cand):
        def body(c, acc):
            off = pl.multiple_of(c * ck, ck)
            m = jnp.where(key_ref[:, pl.ds(off, ck)] >= cand, 1.0, 0.0)
            for q in range(ck // LANES):
                acc = acc + m[:, q * LANES:(q + 1) * LANES]
            return acc
        acc = lax.fori_loop(0, n_past // ck, body, jnp.where(key_new >= cand, 1.0, 0.0))
        return jnp.sum(acc, axis=-1, keepdims=True)

    def bit_step(t, cur):
        cand = cur + lax.shift_left(jnp.int32(1), 31 - t)
        return jnp.where(count_ge(cand) >= n_sel, cand, cur)

    thr = lax.fori_loop(0, 32, bit_step, jnp.full((tp, 1), INT_MIN, I32))
    mnew_ref[0] = jnp.where((key_new >= thr) & (s_new > -jnp.inf), 1.0, 0.0)

    pb = 256
    r_i = lax.broadcasted_iota(I32, (pb, pb), 0)
    c_i = lax.broadcasted_iota(I32, (pb, pb), 1)
    upper = jnp.where(r_i < c_i, 1.0, 0.0).astype(BF16)

    def prefix_block(c, run):
        off = pl.multiple_of(c * pb, pb)
        m = jnp.where(key_ref[:, pl.ds(off, pb)] >= thr, 1.0, 0.0)
        before = _dot(m.astype(BF16), upper) + run
        pfx_ref[:, pl.ds(off, pb)] = jnp.where(m > 0.0, before, -1.0)
        return run + jnp.sum(m, axis=-1, keepdims=True)

    total = lax.fori_loop(0, n_past // pb, prefix_block, jnp.zeros((tp, 1), F32))
    cnt_ref[0] = jnp.broadcast_to(jnp.minimum(total, float(n_sel)), (tp, LANES))

    slot = lax.broadcasted_iota(I32, (n_sel, ck), 0).astype(F32)
    kpos = lax.broadcasted_iota(I32, (n_sel, ck), 1).astype(F32)
    for t in range(t_new):
        def slot_chunk(c, acc, t=t):
            off = pl.multiple_of(c * ck, ck)
            pf = pfx_ref[t:t + 1, pl.ds(off, ck)]
            hit = jnp.where(pf == slot, kpos + jnp.asarray(c * ck, F32), 0.0)
            return acc + jnp.sum(hit, axis=-1, keepdims=True)
        pos = lax.fori_loop(0, n_past // ck, slot_chunk, jnp.zeros((n_sel, 1), F32))
        idx_ref[0, t] = pos.astype(I32)


def dsa_select(iq, iw, ik_new, pool_kidx, page_table, layer, t_new):
    bsz = iq.shape[0]
    n_pages = page_table.shape[1]
    n_past = n_pages * PAGE_SIZE
    n_sel = min(IDX_TOPK, (n_past + t_new) // 4)
    tp = SAMPLE_ROWS
    grid_spec = pltpu.PrefetchScalarGridSpec(
        num_scalar_prefetch=1,
        grid=(bsz,),
        in_specs=[pl.BlockSpec((1, tp * IDX_HEADS, IDX_DIM), lambda b, pt: (b, 0, 0)),
                  pl.BlockSpec((1, tp * IDX_HEADS, 1), lambda b, pt: (b, 0, 0)),
                  pl.BlockSpec((1, tp, IDX_DIM), lambda b, pt: (b, 0, 0)),
                  pl.BlockSpec(memory_space=pl.ANY)],
        out_specs=[pl.BlockSpec((1, t_new, n_sel, 1), lambda b, pt: (b, 0, 0, 0)),
                   pl.BlockSpec((1, tp, LANES), lambda b, pt: (b, 0, 0)),
                   pl.BlockSpec((1, tp, LANES), lambda b, pt: (b, 0, 0))],
        scratch_shapes=[pltpu.VMEM((n_past, IDX_DIM), F32), pltpu.VMEM((tp, n_past), F32),
                        pltpu.VMEM((tp, n_past), I32), pltpu.VMEM((tp, n_past), F32),
                        pltpu.SemaphoreType.DMA(())],
    )
    return pl.pallas_call(
        functools.partial(_dsa_select_kernel, layer, n_pages, n_sel, t_new),
        grid_spec=grid_spec,
        out_shape=[jax.ShapeDtypeStruct((bsz, t_new, n_sel, 1), I32),
                   jax.ShapeDtypeStruct((bsz, tp, LANES), F32),
                   jax.ShapeDtypeStruct((bsz, tp, LANES), F32)],
        compiler_params=_params("arbitrary"),
    )(page_table.reshape(-1), iq, iw, ik_new, pool_kidx)


def _dsa_gather_kernel(layer, n_pages, n_sel, t_new, pt_ref, idx_ref, cnt_ref, mnew_ref, q_ref, kn_ref, vn_ref,
                       k_hbm, v_hbm, o_ref, kbuf, vbuf, ksem, vsem):
    b = pl.program_id(0)
    t = pl.program_id(1)
    base = (b * t_new + t) * n_sel

    def copies(j):
        pos = idx_ref[base + j]
        page = pt_ref[b * n_pages + pos // PAGE_SIZE]
        row = pos % PAGE_SIZE
        return (pltpu.make_async_copy(k_hbm.at[layer, page, pl.ds(row, 1)], kbuf.at[pl.ds(j, 1)], ksem),
                pltpu.make_async_copy(v_hbm.at[layer, page, pl.ds(row, 1)], vbuf.at[pl.ds(j, 1)], vsem))

    def start(j, c):
        ck, cv = copies(j)
        ck.start()
        cv.start()
        return c

    def wait(j, c):
        ck, cv = copies(j)
        ck.wait()
        cv.wait()
        return c

    lax.fori_loop(0, n_sel, start, 0)
    lax.fori_loop(0, n_sel, wait, 0)

    q = q_ref[0]
    scale = HEAD_DIM ** -0.5
    s = jnp.sum(kbuf[...] * q, axis=-1, keepdims=True) * scale
    slot = lax.broadcasted_iota(I32, s.shape, 0)
    s = jnp.where(slot < cnt_ref[b * t_new + t], s, NEG_BIG)
    kn = kn_ref[0]
    s_n = jnp.sum(kn * q, axis=-1, keepdims=True) * scale
    tn = lax.broadcasted_iota(I32, s_n.shape, 0)
    ok = jnp.zeros(s_n.shape, F32)
    for j in range(t_new):
        ok = jnp.where(tn == j, mnew_ref[(b * t_new + t) * t_new + j].astype(F32), ok)
    s_n = jnp.where(ok > 0.0, s_n, NEG_BIG)
    m = jnp.maximum(jnp.max(s, axis=0, keepdims=True), jnp.max(s_n, axis=0, keepdims=True))
    p = jnp.exp(s - m)
    p_n = jnp.exp(s_n - m)
    l = jnp.sum(p, axis=0, keepdims=True) + jnp.sum(p_n, axis=0, keepdims=True)
    acc = jnp.sum(p * vbuf[...], axis=0, keepdims=True) + jnp.sum(p_n * vn_ref[0], axis=0, keepdims=True)
    o_ref[0] = acc / l


def dsa_gather(q, k_new, v_new, idx, cnt, mnew, pool_k, pool_v, page_table, layer):
    bsz, t_new, n_heads, dh = q.shape
    n_pages = page_table.shape[1]
    n_sel = idx.shape[0] // (bsz * t_new)
    grid_spec = pltpu.PrefetchScalarGridSpec(
        num_scalar_prefetch=4,
        grid=(bsz, t_new),
        in_specs=[pl.BlockSpec((1, 1, n_heads, dh), lambda b, t, *_: (b, t, 0, 0)),
                  pl.BlockSpec((1, t_new, n_heads, dh), lambda b, t, *_: (b, 0, 0, 0)),
                  pl.BlockSpec((1, t_new, n_heads, dh), lambda b, t, *_: (b, 0, 0, 0)),
                  pl.BlockSpec(memory_space=pl.ANY), pl.BlockSpec(memory_space=pl.ANY)],
        out_specs=pl.BlockSpec((1, 1, n_heads, dh), lambda b, t, *_: (b, t, 0, 0)),
        scratch_shapes=[pltpu.VMEM((n_sel, n_heads, dh), F32), pltpu.VMEM((n_sel, n_heads, dh), F32),
                        pltpu.SemaphoreType.DMA(()), pltpu.SemaphoreType.DMA(())],
    )
    return pl.pallas_call(
        functools.partial(_dsa_gather_kernel, layer, n_pages, n_sel, t_new),
        grid_spec=grid_spec,
        out_shape=jax.ShapeDtypeStruct((bsz, t_new, n_heads, dh), F32),
        compiler_params=_params("arbitrary", "arbitrary"),
    )(page_table.reshape(-1), idx, cnt, mnew, q, k_new, v_new, pool_k, pool_v)


def dsa_sample(proj, ikiw, pool_k, pool_v, pool_kidx, page_table, layer, bsz, t_new):
    aw = pool_k.shape[3] * pool_k.shape[4]
    n_heads = pool_k.shape[3]
    tp = SAMPLE_ROWS
    hs = (bsz, t_new, n_heads, HEAD_DIM)
    pad_t = lambda a: jnp.concatenate([a, jnp.zeros((bsz, tp - t_new) + a.shape[2:], a.dtype)], axis=1)
    iq = pad_t(proj[:, 3 * aw:3 * aw + IDX_HEADS * IDX_DIM].reshape(bsz, t_new, IDX_HEADS, IDX_DIM))
    iw = pad_t(ikiw[:, IDX_DIM:IDX_DIM + IDX_HEADS].reshape(bsz, t_new, IDX_HEADS))
    ik_new = pad_t(ikiw[:, :IDX_DIM].reshape(bsz, t_new, IDX_DIM))
    idx, cnt, mnew = dsa_select(iq.reshape(bsz, tp * IDX_HEADS, IDX_DIM), iw.reshape(bsz, tp * IDX_HEADS, 1),
                                ik_new, pool_kidx, page_table, layer, t_new)
    cnt = cnt[:, :t_new, 0].astype(I32).reshape(-1)
    mnew = mnew[:, :t_new, :t_new].astype(I32).reshape(-1)
    out = dsa_gather(proj[:, :aw].reshape(hs), proj[:, aw:2 * aw].reshape(hs), proj[:, 2 * aw:3 * aw].reshape(hs),
                     idx.reshape(-1), cnt, mnew, pool_k, pool_v, page_table, layer)
    return out.reshape(bsz * t_new, aw)


def _moba_mean_kernel(ppb, pt_ref, *refs):
    pages, o_ref = refs[:ppb], refs[ppb]
    acc = jnp.sum(pages[0][0, 0], axis=0)
    for pg in pages[1:]:
        acc = acc + jnp.sum(pg[0, 0], axis=0)
    o_ref[0, 0] = acc * (1.0 / MOBA_BLOCK)


def moba_block_means(pool_k, page_table, layer):
    bsz, n_pages = page_table.shape
    _, _, page, n_heads, dh = pool_k.shape
    ppb = MOBA_BLOCK // page
    nbp = n_pages // ppb

    def page_spec(j):
        return pl.BlockSpec((1, 1, page, n_heads, dh),
                            lambda b, n, pt: (layer, pt[b * n_pages + n * ppb + j], 0, 0, 0))

    grid_spec = pltpu.PrefetchScalarGridSpec(
        num_scalar_prefetch=1,
        grid=(bsz, nbp),
        in_specs=[page_spec(j) for j in range(ppb)],
        out_specs=pl.BlockSpec((1, 1, n_heads, dh), lambda b, n, pt: (b, n, 0, 0)),
    )
    return pl.pallas_call(
        functools.partial(_moba_mean_kernel, ppb),
        grid_spec=grid_spec,
        out_shape=jax.ShapeDtypeStruct((bsz, nbp, n_heads, dh), F32),
        compiler_params=_params("parallel", "arbitrary"),
    )(page_table.reshape(-1), *([pool_k] * ppb))


def _moba_pick_kernel(n_top, km_ref, q_ref, sel_ref):
    km = km_ref[0]
    nbp = km.shape[0]
    t_new = q_ref.shape[1]
    for t in range(t_new):
        s = jnp.sum(km * q_ref[0, t:t + 1], axis=-1, keepdims=True)
        blk = lax.broadcasted_iota(I32, s.shape, 0)
        for j in range(n_top):
            top = jnp.max(s, axis=0, keepdims=True)
            first = jnp.min(jnp.where(s == top, blk, nbp), axis=0, keepdims=True)
            sel_ref[0, t, j] = first[0]
            s = jnp.where(blk == first, -jnp.inf, s)


def moba_pick(kmean, q, n_top):
    bsz, nbp, n_heads, dh = kmean.shape
    t_new = q.shape[1]
    return pl.pallas_call(
        functools.partial(_moba_pick_kernel, n_top),
        grid=(bsz,),
        in_specs=[pl.BlockSpec((1, nbp, n_heads, dh), lambda b: (b, 0, 0, 0)),
                  pl.BlockSpec((1, t_new, n_heads, dh), lambda b: (b, 0, 0, 0))],
        out_specs=pl.BlockSpec((1, t_new, n_top, n_heads, 1), lambda b: (b, 0, 0, 0, 0)),
        out_shape=jax.ShapeDtypeStruct((bsz, t_new, n_top, n_heads, 1), I32),
        compiler_params=_params("parallel"),
    )(kmean, q)


def _moba_gather_kernel(layer, n_pages, n_top, ppb, pt_ref, sel_ref, q_ref, kn_ref, vn_ref, k_hbm, v_hbm, o_ref,
                        kbuf, vbuf, ksem, vsem):
    b = pl.program_id(0)
    t = pl.program_id(1)
    t_new = pl.num_programs(1)
    n_heads = kbuf.shape[0]
    page = PAGE_SIZE
    copies = []
    for h in range(n_heads):
        for j in range(n_top):
            blk = sel_ref[((b * t_new + t) * n_top + j) * n_heads + h]
            for pg in range(ppb):
                phys = pt_ref[b * n_pages + blk * ppb + pg]
                dst = pl.ds((j * ppb + pg) * page, page)
                copies.append(pltpu.make_async_copy(k_hbm.at[layer, phys, :, h, :], kbuf.at[h, dst], ksem))
                copies.append(pltpu.make_async_copy(v_hbm.at[layer, phys, :, h, :], vbuf.at[h, dst], vsem))
    for c in copies:
        c.start()
    for c in copies:
        c.wait()

    q = q_ref[0, 0]
    scale = HEAD_DIM ** -0.5
    s = jnp.sum(kbuf[...] * q, axis=-1, keepdims=True) * scale
    s_n = jnp.sum(kn_ref[0] * q, axis=-1, keepdims=True) * scale
    tn = lax.broadcasted_iota(I32, s_n.shape, 1)
    s_n = jnp.where(tn <= t, s_n, NEG_BIG)
    m = jnp.maximum(jnp.max(s, axis=1, keepdims=True), jnp.max(s_n, axis=1, keepdims=True))
    p = jnp.exp(s - m)
    p_n = jnp.exp(s_n - m)
    l = jnp.sum(p, axis=1, keepdims=True) + jnp.sum(p_n, axis=1, keepdims=True)
    acc = jnp.sum(p * vbuf[...], axis=1, keepdims=True) + jnp.sum(p_n * vn_ref[0], axis=1, keepdims=True)
    o_ref[0, 0] = acc / l


def moba_gather(q, k_new, v_new, sel, pool_k, pool_v, page_table, layer, n_top):
    bsz, t_new, n_heads, _, dh = q.shape
    n_pages = page_table.shape[1]
    ppb = MOBA_BLOCK // PAGE_SIZE
    grid_spec = pltpu.PrefetchScalarGridSpec(
        num_scalar_prefetch=2,
        grid=(bsz, t_new),
        in_specs=[pl.BlockSpec((1, 1, n_heads, 1, dh), lambda b, t, *_: (b, t, 0, 0, 0)),
                  pl.BlockSpec((1, n_heads, t_new, dh), lambda b, t, *_: (b, 0, 0, 0)),
                  pl.BlockSpec((1, n_heads, t_new, dh), lambda b, t, *_: (b, 0, 0, 0)),
                  pl.BlockSpec(memory_space=pl.ANY), pl.BlockSpec(memory_space=pl.ANY)],
        out_specs=pl.BlockSpec((1, 1, n_heads, 1, dh), lambda b, t, *_: (b, t, 0, 0, 0)),
        scratch_shapes=[pltpu.VMEM((n_heads, n_top * MOBA_BLOCK, dh), F32),
                        pltpu.VMEM((n_heads, n_top * MOBA_BLOCK, dh), F32),
                        pltpu.SemaphoreType.DMA(()), pltpu.SemaphoreType.DMA(())],
    )
    return pl.pallas_call(
        functools.partial(_moba_gather_kernel, layer, n_pages, n_top, ppb),
        grid_spec=grid_spec,
        out_shape=jax.ShapeDtypeStruct((bsz, t_new, n_heads, 1, dh), F32),
        compiler_params=_params("arbitrary", "arbitrary"),
    )(page_table.reshape(-1), sel, q, k_new, v_new, pool_k, pool_v)


def moba_sample(proj, pool_k, pool_v, page_table, layer, bsz, t_new):
    n_heads = pool_k.shape[3]
    aw = n_heads * HEAD_DIM
    n_past = page_table.shape[1] * PAGE_SIZE
    assert n_past % MOBA_BLOCK == 0 and t_new <= MOBA_BLOCK
    n_top = min(MOBA_TOPK, n_past // MOBA_BLOCK)
    hs = (bsz, t_new, n_heads, HEAD_DIM)
    q = proj[:, aw:2 * aw].reshape(hs)
    k_new = proj[:, 2 * aw:3 * aw].reshape(hs).transpose(0, 2, 1, 3)
    v_new = proj[:, 3 * aw:4 * aw].reshape(hs).transpose(0, 2, 1, 3)
    kmean = moba_block_means(pool_k, page_table, layer)
    sel = moba_pick(kmean, q, n_top)
    out = moba_gather(q[:, :, :, None, :], k_new, v_new, sel.reshape(-1), pool_k, pool_v, page_table, layer, n_top)
    return out.reshape(bsz * t_new, aw)


def _ab_weight(w):
    d = w.shape[0]
    aw = (w.shape[1] - IDX_HEADS * IDX_DIM - IDX_DIM - IDX_HEADS) // 6
    main = 3 * aw + IDX_HEADS * IDX_DIM
    small = IDX_DIM + IDX_HEADS
    pad = 512 - small
    return _split_stack(jnp.concatenate([w[:, :main], w[:, main + small:], w[:, main:main + small],
                                         jnp.zeros((d, pad), w.dtype)], axis=1))


def _router_weight(wg, bg, we, be):
    d = wg.shape[0]
    pad = LANES - N_GROUPS - N_EXPERTS
    w = jnp.concatenate([wg, we, jnp.zeros((d, pad), F32)], axis=1)
    b = jnp.concatenate([bg, be, jnp.zeros((pad,), F32)]).reshape(1, LANES)
    return w, b


def _trunk(x, mod, p, past, prompt):
    bsz, t, d = x.shape
    n = bsz * t
    depth = p['norm1'].shape[0]
    aw = d // 2
    n_heads = aw // HEAD_DIM
    if prompt:
        tm, rows_per_mod, tm_e = 512, t, 256
    else:
        tm, rows_per_mod, tm_e = n, n, 16
    xf = x.reshape(n, d)
    ab_states, cd_states = [], []
    pos = (0 if prompt else past['page_table'].shape[1] * PAGE_SIZE) + jnp.arange(t)
    for l in range(depth):
        i = l // 2
        m6 = mod[l].reshape(bsz, 6, d).transpose(1, 0, 2)
        if prompt:
            mod6 = m6[:, :, None, :]
        else:
            mod6 = jnp.repeat(m6, t, axis=1)[:, None, :, :]
        if l % 2 == 0:
            proj = norm_proj(xf, p['norm1'][l], mod6, 1, 0, _ab_weight(p['w_in_ab'][i]), tm, rows_per_mod)
            k = proj[:, aw:2 * aw]
            v = proj[:, 2 * aw:3 * aw]
            ikiw = proj[:, 7 * aw:7 * aw + LANES]
            ik = ikiw[:, :IDX_DIM]
            if prompt:
                zeros = jnp.zeros_like(ik)
                ik_e = _split_stack(jnp.concatenate([ik, zeros], axis=1))
                ik_o = _split_stack(jnp.concatenate([zeros, ik], axis=1))
                a = dsa_prompt(proj, ik_e, ik_o, n_heads, bsz, t, 256, 512)
                b, nbuf = conv_prompt(proj, p['conv_w'][i], bsz, t, 512)
                new_buf = nbuf[:, 8 - (CONV_W - 1):]
            else:
                a = dsa_sample(proj, ikiw, past['cache_a_k'], past['cache_a_v'], past['cache_a_kidx'],
                               past['page_table'], i, bsz, t)
                b, u = conv_sample(proj, p['conv_w'][i], past['state_b_conv'][i], t)
                new_buf = u.reshape(bsz, t, aw)[:, t - (CONV_W - 1):]
            xf = out_proj(a, b, p['w_out_ab'][i], xf, mod6, 2, tm, rows_per_mod)
            ab_states.append((k.reshape(bsz, t, n_heads, HEAD_DIM), v.reshape(bsz, t, n_heads, HEAD_DIM),
                              ik.reshape(bsz, t, IDX_DIM), new_buf))
        else:
            proj = norm_proj(xf, p['norm1'][l], mod6, 1, 0, _split_stack(p['w_in_cd'][i]), tm, rows_per_mod)
            k = proj[:, 2 * aw:3 * aw]
            v = proj[:, 3 * aw:4 * aw]
            chunk = 8 if prompt else t
            wts = s5_weights(p['s5_log_dt'][i], p['s5_a_re'][i], p['s5_a_im'][i], p['s5_b_re'][i], p['s5_b_im'][i],
                             p['s5_c_re'][i], p['s5_c_im'][i], chunk)
            nblk = aw // LANES
            if prompt:
                h0 = jnp.zeros((nblk, bsz, 1, (aw // S5_GROUP) * S5_STATE // nblk), F32)
                y, hr, hi = s5_prompt(proj, h0, h0, wts, p['s5_d'][i], bsz, t, chunk, 64)
                hr, hi = hr[:, :, 0], hi[:, :, 0]
                dd = moba_prompt(proj, bsz, t)
            else:
                to_blk = lambda s: s.reshape(bsz, nblk, -1).transpose(1, 0, 2)
                y, hr, hi = s5_sample(proj, to_blk(past['state_c_re'][i]), to_blk(past['state_c_im'][i]), wts,
                                      p['s5_d'][i], bsz, chunk)
                dd = moba_sample(proj, past['cache_d_k'], past['cache_d_v'], past['page_table'], i, bsz, t)
            from_blk = lambda s: s.transpose(1, 0, 2).reshape(bsz, aw // S5_GROUP, S5_STATE)
            c = glu(y, p['glu_w'][i], p['glu_b'][i], tm)
            xf = out_proj(c, dd, p['w_out_cd'][i], xf, mod6, 2, tm, rows_per_mod)
            cd_states.append((from_blk(hr), from_blk(hi), k.reshape(bsz, t, n_heads, HEAD_DIM),
                              v.reshape(bsz, t, n_heads, HEAD_DIM)))
        w_r, b_r = _router_weight(p['router_g_w'][l], p['router_g_b'][l], p['router_e_w'][l], p['router_e_b'][l])
        xf = hier_moe(xf, p['norm2'][l], mod6, w_r, b_r, p['exp_w1'], p['exp_w3'], p['exp_w2'], l, tm,
                      rows_per_mod, tm_e)
    y = final_norm(xf, p['final_norm'], tm).reshape(bsz, t, d)
    a_k, a_v, a_kidx, b_conv = [jnp.stack(s) for s in zip(*ab_states)]
    c_re, c_im, d_k, d_v = [jnp.stack(s) for s in zip(*cd_states)]
    return y, a_k, a_v, a_kidx, b_conv, c_re, c_im, d_k, d_v


def kernel(x_prompt, x_sample, cache_a_k, cache_a_v, cache_a_kidx, state_b_conv, state_c_re, state_c_im, cache_d_k, cache_d_v, page_table, c_prompt, c_sample, norm1, norm2, ada_w, ada_b, w_in_ab, conv_w, w_out_ab, w_in_cd, s5_a_re, s5_a_im, s5_b_re, s5_b_im, s5_c_re, s5_c_im, s5_d, s5_log_dt, glu_w, glu_b, w_out_cd, router_g_w, router_g_b, router_e_w, router_e_b, exp_w1, exp_w3, exp_w2, final_norm):
    p = dict(norm1=norm1, norm2=norm2, w_in_ab=w_in_ab, conv_w=conv_w, w_out_ab=w_out_ab, w_in_cd=w_in_cd,
             s5_a_re=s5_a_re, s5_a_im=s5_a_im, s5_b_re=s5_b_re, s5_b_im=s5_b_im, s5_c_re=s5_c_re, s5_c_im=s5_c_im,
             s5_d=s5_d, s5_log_dt=s5_log_dt, glu_w=glu_w, glu_b=glu_b, w_out_cd=w_out_cd, router_g_w=router_g_w,
             router_g_b=router_g_b, router_e_w=router_e_w, router_e_b=router_e_b, exp_w1=exp_w1, exp_w3=exp_w3,
             exp_w2=exp_w2, final_norm=final_norm)
    past = dict(cache_a_k=cache_a_k, cache_a_v=cache_a_v, cache_a_kidx=cache_a_kidx, state_b_conv=state_b_conv,
                state_c_re=state_c_re, state_c_im=state_c_im, cache_d_k=cache_d_k, cache_d_v=cache_d_v,
                page_table=page_table)
    bp, bs = c_prompt.shape[0], c_sample.shape[0]
    rows = -(-(bp + bs) // 8) * 8
    c_all = jnp.concatenate([c_prompt, c_sample, jnp.zeros((rows - bp - bs, c_prompt.shape[1]), F32)], axis=0)
    mod = ada_mod(c_all, ada_w, ada_b)
    outs_p = _trunk(x_prompt, mod[:, :bp], p, None, True)
    outs_s = _trunk(x_sample, mod[:, bp:bp + bs], p, past, False)
    return (outs_p[0], outs_s[0]) + tuple(outs_p[1:]) + tuple(outs_s[1:])
```

```python
import functools
import math

import jax
import jax.numpy as jnp
import numpy as np
from jax import lax
from jax.experimental import pallas as pl
from jax.experimental.pallas import tpu as pltpu

F32 = jnp.float32
BF16 = jnp.bfloat16
I32 = jnp.int32

HEAD_DIM = 128
IDX_HEADS = 16
IDX_DIM = 64
IDX_TOPK = 256
CONV_W = 3
S5_GROUP = 16
S5_STATE = 64
MOBA_BLOCK = 256
MOBA_TOPK = 3
N_GROUPS = 4
EXP_PER_GROUP = 8
N_EXPERTS = N_GROUPS * EXP_PER_GROUP
MOE_TOPK = 2
PAGE_SIZE = 128
EPS = 1e-6

LANES = 128
NEG_BIG = -1e30
INT_MIN = -(2 ** 31)
VMEM_LIMIT = 56 * 1024 * 1024


def _params(*sem):
    return pltpu.CompilerParams(dimension_semantics=sem, vmem_limit_bytes=VMEM_LIMIT)


def _dot(a, b):
    return jnp.dot(a, b, preferred_element_type=F32)


def _dot_nt(a, b):
    return lax.dot_general(a, b, (((1,), (1,)), ((), ())), preferred_element_type=F32)


def _split(x):
    hi = x.astype(BF16)
    return hi, (x - hi.astype(F32)).astype(BF16)


def _split_stack(x):
    x = x.astype(F32)
    hi = lax.reduce_precision(x, exponent_bits=8, mantissa_bits=7)
    return jnp.stack([hi.astype(BF16), (x - hi).astype(BF16)])


def _dot3(a, b, dot=_dot):
    ah, al = a if isinstance(a, tuple) else _split(a)
    bh, bl = b if isinstance(b, tuple) else _split(b)
    return dot(ah, bh) + dot(ah, bl) + dot(al, bh)


def _dot3_nt(a, b):
    return _dot3(a, b, _dot_nt)


def _pair(ref, *idx):
    return ref[(0,) + idx], ref[(1,) + idx]


def _ada_kernel(c_ref, w_ref, b_ref, o_ref):
    c = c_ref[...]
    o_ref[0] = _dot3(c * jax.nn.sigmoid(c), w_ref[0]) + b_ref[0]


def ada_mod(c, ada_w, ada_b):
    depth, d, n6 = ada_w.shape
    r = c.shape[0]
    tn = 1024
    return pl.pallas_call(
        _ada_kernel,
        grid=(depth, n6 // tn),
        in_specs=[pl.BlockSpec((r, d), lambda l, j: (0, 0)),
                  pl.BlockSpec((1, d, tn), lambda l, j: (l, 0, j)),
                  pl.BlockSpec((1, 1, tn), lambda l, j: (l, 0, j))],
        out_specs=pl.BlockSpec((1, r, tn), lambda l, j: (l, 0, j)),
        out_shape=jax.ShapeDtypeStruct((depth, r, n6), F32),
        compiler_params=_params("parallel", "parallel"),
    )(c, ada_w, ada_b.reshape(depth, 1, n6))


def _norm_mod(x, g, sc, sh):
    y = x * lax.rsqrt(jnp.mean(x * x, axis=-1, keepdims=True) + EPS)
    return (y * g) * (1.0 + sc) + sh


def _norm_proj_kernel(x_ref, g_ref, sc_ref, sh_ref, w_ref, o_ref, hn_ref):
    @pl.when(pl.program_id(1) == 0)
    def _():
        hi, lo = _split(_norm_mod(x_ref[...], g_ref[...], sc_ref[0, 0], sh_ref[0, 0]))
        hn_ref[0] = hi
        hn_ref[1] = lo

    o_ref[...] = _dot3(_pair(hn_ref), _pair(w_ref))


def _mod_spec(which, tm, rows_per_mod, rm, d, nargs):
    if nargs == 2:
        return pl.BlockSpec((1, 1, rm, d), lambda i, j: (which, (i * tm) // rows_per_mod, 0, 0))
    return pl.BlockSpec((1, 1, rm, d), lambda i: (which, (i * tm) // rows_per_mod, 0, 0))


def norm_proj(x, g, mod6, sc_idx, sh_idx, w, tm, rows_per_mod):
    n, d = x.shape
    nc = w.shape[2]
    tn = 512
    rm = mod6.shape[2]
    return pl.pallas_call(
        _norm_proj_kernel,
        grid=(n // tm, nc // tn),
        in_specs=[pl.BlockSpec((tm, d), lambda i, j: (i, 0)),
                  pl.BlockSpec((1, d), lambda i, j: (0, 0)),
                  _mod_spec(sc_idx, tm, rows_per_mod, rm, d, 2),
                  _mod_spec(sh_idx, tm, rows_per_mod, rm, d, 2),
                  pl.BlockSpec((2, d, tn), lambda i, j: (0, 0, j))],
        out_specs=pl.BlockSpec((tm, tn), lambda i, j: (i, j)),
        out_shape=jax.ShapeDtypeStruct((n, nc), F32),
        scratch_shapes=[pltpu.VMEM((2, tm, d), BF16)],
        compiler_params=_params("parallel", "arbitrary"),
    )(x, g.reshape(1, d), mod6, mod6, w)


def _out_proj_kernel(a_ref, b_ref, wa_ref, wb_ref, x_ref, g_ref, o_ref):
    y = _dot3(a_ref[...], _pair(wa_ref)) + _dot3(b_ref[...], _pair(wb_ref))
    o_ref[...] = x_ref[...] + g_ref[0, 0] * y


def out_proj(a, b, w, x, mod6, g_idx, tm, rows_per_mod):
    n, d = x.shape
    ka, kb = a.shape[1], b.shape[1]
    rm = mod6.shape[2]
    tn = d // 2
    wa, wb = _split_stack(w[:ka]), _split_stack(w[ka:])
    return pl.pallas_call(
        _out_proj_kernel,
        grid=(n // tm, d // tn),
        in_specs=[pl.BlockSpec((tm, ka), lambda i, j: (i, 0)),
                  pl.BlockSpec((tm, kb), lambda i, j: (i, 0)),
                  pl.BlockSpec((2, ka, tn), lambda i, j: (0, 0, j)),
                  pl.BlockSpec((2, kb, tn), lambda i, j: (0, 0, j)),
                  pl.BlockSpec((tm, tn), lambda i, j: (i, j)),
                  pl.BlockSpec((1, 1, rm, tn), lambda i, j: (g_idx, (i * tm) // rows_per_mod, 0, j))],
        out_specs=pl.BlockSpec((tm, tn), lambda i, j: (i, j)),
        out_shape=jax.ShapeDtypeStruct((n, d), F32),
        compiler_params=_params("parallel", "parallel"),
    )(a, b, wa, wb, x, mod6)


def _final_norm_kernel(x_ref, g_ref, o_ref):
    x = x_ref[...]
    o_ref[...] = (x * lax.rsqrt(jnp.mean(x * x, axis=-1, keepdims=True) + EPS)) * g_ref[...]


def final_norm(x, g, tm):
    n, d = x.shape
    return pl.pallas_call(
        _final_norm_kernel,
        grid=(n // tm,),
        in_specs=[pl.BlockSpec((tm, d), lambda i: (i, 0)), pl.BlockSpec((1, d), lambda i: (0, 0))],
        out_specs=pl.BlockSpec((tm, d), lambda i: (i, 0)),
        out_shape=jax.ShapeDtypeStruct((n, d), F32),
        compiler_params=_params("parallel"),
    )(x, g.reshape(1, d))


def _conv_prompt_kernel(gb_ref, gc_ref, gh_ref, pc_ref, ph_ref, w_ref, o_ref, nb_ref):
    j = pl.program_id(1)
    tq = gc_ref.shape[0]
    u = gc_ref[...] * gh_ref[...]
    prev = pc_ref[...] * ph_ref[...]
    prev = jnp.where(j == 0, 0.0, prev)
    row = lax.broadcasted_iota(I32, u.shape, 0)
    u1 = jnp.where(row == 0, prev[7:8], pltpu.roll(u, 1, 0))
    u2 = jnp.where(row == 0, prev[6:7], jnp.where(row == 1, prev[7:8], pltpu.roll(u, 2, 0)))
    y = w_ref[0:1] * u2 + w_ref[1:2] * u1 + w_ref[2:3] * u
    o_ref[...] = (gb_ref[...] * y).astype(o_ref.dtype)
    nb_ref[0] = u[tq - 8:tq]


def conv_prompt(proj, conv_w, bsz, t, tq):
    n = proj.shape[0]
    w = conv_w.shape[1]
    nt = t // tq
    wpad = jnp.concatenate([conv_w, jnp.zeros((8 - CONV_W, w), F32)], axis=0)

    def halo(cb):
        return pl.BlockSpec((8, w), lambda b, j: (jnp.maximum((b * t + j * tq) // 8 - 1, 0), cb))

    return pl.pallas_call(
        _conv_prompt_kernel,
        grid=(bsz, nt),
        in_specs=[pl.BlockSpec((tq, w), lambda b, j: (b * nt + j, 4)),
                  pl.BlockSpec((tq, w), lambda b, j: (b * nt + j, 5)),
                  pl.BlockSpec((tq, w), lambda b, j: (b * nt + j, 6)),
                  halo(5), halo(6),
                  pl.BlockSpec((8, w), lambda b, j: (0, 0))],
        out_specs=[pl.BlockSpec((tq, w), lambda b, j: (b * nt + j, 0)),
                   pl.BlockSpec((1, 8, w), lambda b, j: (b, 0, 0))],
        out_shape=[jax.ShapeDtypeStruct((n, w), F32), jax.ShapeDtypeStruct((bsz, 8, w), F32)],
        compiler_params=_params("parallel", "arbitrary"),
    )(proj, proj, proj, proj, proj, wpad)


def _conv_sample_kernel(t_len, gb_ref, gc_ref, gh_ref, p0_ref, p1_ref, w_ref, o_ref, u_ref):
    u = gc_ref[...] * gh_ref[...]
    row = lax.broadcasted_iota(I32, u.shape, 0) % t_len
    u1 = jnp.where(row == 0, p1_ref[...], pltpu.roll(u, 1, 0))
    u2 = jnp.where(row == 0, p0_ref[...], jnp.where(row == 1, p1_ref[...], pltpu.roll(u, 2, 0)))
    y = w_ref[0:1] * u2 + w_ref[1:2] * u1 + w_ref[2:3] * u
    o_ref[...] = (gb_ref[...] * y).astype(o_ref.dtype)
    u_ref[...] = u


def conv_sample(proj, conv_w, buf, t_len):
    n = proj.shape[0]
    w = conv_w.shape[1]
    wpad = jnp.concatenate([conv_w, jnp.zeros((8 - CONV_W, w), F32)], axis=0)
    p0 = jnp.repeat(buf[:, 0], t_len, axis=0)
    p1 = jnp.repeat(buf[:, 1], t_len, axis=0)
    full = lambda cb: pl.BlockSpec((n, w), lambda i: (0, cb))
    return pl.pallas_call(
        functools.partial(_conv_sample_kernel, t_len),
        grid=(1,),
        in_specs=[full(4), full(5), full(6), full(0), full(0), pl.BlockSpec((8, w), lambda i: (0, 0))],
        out_specs=[full(0), full(0)],
        out_shape=[jax.ShapeDtypeStruct((n, w), F32), jax.ShapeDtypeStruct((n, w), F32)],
        compiler_params=_params("arbitrary"),
    )(proj, proj, proj, p0, p1, wpad)


def _float_key(s):
    b = pltpu.bitcast(s, I32)
    return b ^ (lax.shift_right_arithmetic(b, 31) & 0x7FFFFFFF)


def _dsa_prompt_kernel(n_sel, tk, q_ref, iq_ref, iw_ref, ike_ref, iko_ref, k_ref, v_ref, o_ref, key_ref, bias_ref):
    i = pl.program_id(1)
    tq = q_ref.shape[0]
    nck = ((i + 1) * tq + tk - 1) // tk

    @pl.when(pl.program_id(2) == 0)
    def _():
        _dsa_prompt_select(n_sel, tk, nck, i, iq_ref, iw_ref, ike_ref, iko_ref, key_ref, bias_ref)

    qh = _split(q_ref[...])

    def att_chunk(c, carry):
        m, l, acc = carry
        off = pl.multiple_of(c * tk, tk)
        s = _dot3_nt(qh, k_ref[pl.ds(off, tk), :]) * HEAD_DIM ** -0.5
        s = jnp.where(bias_ref[:, pl.ds(off, tk)] == 0.0, s, NEG_BIG)
        m_new = jnp.maximum(m, jnp.max(s, axis=-1, keepdims=True))
        alpha = jnp.exp(m - m_new)
        p = jnp.exp(s - m_new)
        l = alpha * l + jnp.sum(p, axis=-1, keepdims=True)
        acc = alpha * acc + _dot3(p, v_ref[pl.ds(off, tk), :])
        return m_new, l, acc

    m, l, acc = lax.fori_loop(
        0, nck, att_chunk,
        (jnp.full((tq, 1), NEG_BIG, F32), jnp.zeros((tq, 1), F32), jnp.zeros((tq, HEAD_DIM), F32)))
    o_ref[...] = acc / l


def _dsa_prompt_select(n_sel, tk, nck, i, iq_ref, iw_ref, ike_ref, iko_ref, key_ref, bias_ref):
    tq = iq_ref.shape[0]
    row = i * tq + lax.broadcasted_iota(I32, (tq, tk), 0)
    col0 = lax.broadcasted_iota(I32, (tq, tk), 1)
    iw = iw_ref[...]

    def score_chunk(c, carry):
        off = pl.multiple_of(c * tk, tk)
        ke = _pair(ike_ref, pl.ds(off, tk))
        ko = _pair(iko_ref, pl.ds(off, tk))
        acc = jnp.zeros((tq, tk), F32)
        for hp in range(IDX_HEADS // 2):
            iq2 = _split(iq_ref[:, hp * LANES:(hp + 1) * LANES])
            s0 = _dot3_nt(iq2, ke) * IDX_DIM ** -0.5
            s1 = _dot3_nt(iq2, ko) * IDX_DIM ** -0.5
            acc = acc + jnp.maximum(s0, 0.0) * iw[:, IDX_DIM + 2 * hp:IDX_DIM + 2 * hp + 1]
            acc = acc + jnp.maximum(s1, 0.0) * iw[:, IDX_DIM + 2 * hp + 1:IDX_DIM + 2 * hp + 2]
        acc = acc * IDX_HEADS ** -0.5
        acc = jnp.where(off + col0 <= row, acc, -jnp.inf)
        key_ref[:, pl.ds(off, tk)] = _float_key(acc)
        return carry

    lax.fori_loop(0, nck, score_chunk, 0)

    def count_ge(cand):
        def body(c, acc):
            off = pl.multiple_of(c * tk, tk)
            m = jnp.where(key_ref[:, pl.ds(off, tk)] >= cand, 1.0, 0.0)
            for q in range(tk // LANES):
                acc = acc + m[:, q * LANES:(q + 1) * LANES]
            return acc
        acc = lax.fori_loop(0, nck, body, jnp.zeros((tq, LANES), F32))
        return jnp.sum(acc, axis=-1, keepdims=True)

    def bit_step(t, cur):
        cand = cur + lax.shift_left(jnp.int32(1), 31 - t)
        return jnp.where(count_ge(cand) >= n_sel, cand, cur)

    thr = lax.fori_loop(0, 32, bit_step, jnp.full((tq, 1), INT_MIN, I32))

    def bias_chunk(c, carry):
        off = pl.multiple_of(c * tk, tk)
        sel = (key_ref[:, pl.ds(off, tk)] >= thr) & (off + col0 <= row)
        bias_ref[:, pl.ds(off, tk)] = jnp.where(sel, 0.0, NEG_BIG)
        return carry

    lax.fori_loop(0, nck, bias_chunk, 0)


def dsa_prompt(proj, ik_e, ik_o, n_heads, bsz, t, tq, tk):
    n = proj.shape[0]
    nt = t // tq
    n_sel = min(IDX_TOPK, t // 4)
    return pl.pallas_call(
        functools.partial(_dsa_prompt_kernel, n_sel, tk),
        grid=(bsz, nt, n_heads),
        in_specs=[pl.BlockSpec((tq, HEAD_DIM), lambda b, i, h: (b * nt + i, h)),
                  pl.BlockSpec((tq, IDX_HEADS * IDX_DIM), lambda b, i, h: (b * nt + i, 3)),
                  pl.BlockSpec((tq, LANES), lambda b, i, h: (b * nt + i, 7 * n_heads)),
                  pl.BlockSpec((2, t, LANES), lambda b, i, h: (0, b, 0)),
                  pl.BlockSpec((2, t, LANES), lambda b, i, h: (0, b, 0)),
                  pl.BlockSpec((t, HEAD_DIM), lambda b, i, h: (b, n_heads + h)),
                  pl.BlockSpec((t, HEAD_DIM), lambda b, i, h: (b, 2 * n_heads + h))],
        out_specs=pl.BlockSpec((tq, HEAD_DIM), lambda b, i, h: (b * nt + i, h)),
        out_shape=jax.ShapeDtypeStruct((n, n_heads * HEAD_DIM), F32),
        scratch_shapes=[pltpu.VMEM((tq, t), I32), pltpu.VMEM((tq, t), F32)],
        compiler_params=_params("parallel", "arbitrary", "arbitrary"),
    )(proj, proj, proj, ik_e, ik_o, proj, proj)


def s5_weights(log_dt, a_re, a_im, b_re, b_im, c_re, c_im, chunk):
    hp = lax.Precision.HIGHEST
    g, p = a_re.shape
    dt = jnp.exp(log_dt.astype(F32))[:, None]
    lr, li = a_re.astype(F32), a_im.astype(F32)
    mag = jnp.exp(lr * dt)
    ab_re, ab_im = mag * jnp.cos(li * dt), mag * jnp.sin(li * dt)
    den = lr * lr + li * li
    f_re = ((ab_re - 1.0) * lr + ab_im * li) / den
    f_im = (ab_im * lr - (ab_re - 1.0) * li) / den
    br, bi = b_re.astype(F32), b_im.astype(F32)
    bb_re = f_re[..., None] * br - f_im[..., None] * bi
    bb_im = f_re[..., None] * bi + f_im[..., None] * br
    pr, pi = [jnp.ones_like(ab_re)], [jnp.zeros_like(ab_im)]
    for _ in range(chunk):
        pr.append(pr[-1] * ab_re - pi[-1] * ab_im)
        pi.append(pr[-2] * ab_im + pi[-1] * ab_re)
    pw_re, pw_im = jnp.stack(pr), jnp.stack(pi)
    cr, ci = c_re.astype(F32), c_im.astype(F32)
    rev_re = jnp.stack(pr[chunk - 1::-1])
    rev_im = jnp.stack(pi[chunk - 1::-1])
    w_re = rev_re[..., None] * bb_re[None] - rev_im[..., None] * bb_im[None]
    w_im = rev_re[..., None] * bb_im[None] + rev_im[..., None] * bb_re[None]
    z_re = cr[None] * pw_re[1:, :, None, :] - ci[None] * pw_im[1:, :, None, :]
    z_im = cr[None] * pw_im[1:, :, None, :] + ci[None] * pw_re[1:, :, None, :]
    ab_b_re = pw_re[:chunk, :, :, None] * bb_re[None] - pw_im[:chunk, :, :, None] * bb_im[None]
    ab_b_im = pw_re[:chunk, :, :, None] * bb_im[None] + pw_im[:chunk, :, :, None] * bb_re[None]
    k_tap = (jnp.einsum('gop,tgpi->tgoi', cr, ab_b_re, precision=hp)
             - jnp.einsum('gop,tgpi->tgoi', ci, ab_b_im, precision=hp))

    nblk = g * S5_GROUP // LANES
    gpb = g // nblk
    eye = jnp.eye(gpb, dtype=F32)

    def blockdiag(m, rows_last):
        c_, _, r_, k_ = m.shape
        m5 = m.reshape(c_, nblk, gpb, r_, k_)
        out = m5[:, :, :, :, None, :] * eye[None, None, :, None, :, None]
        return out.transpose(1, 0, 2, 3, 4, 5).reshape(nblk, c_, gpb * r_, gpb * k_)

    w_re_b = _split_stack(blockdiag(jnp.swapaxes(w_re, 2, 3), None))
    w_im_b = _split_stack(blockdiag(jnp.swapaxes(w_im, 2, 3), None))
    v_re_b = _split_stack(blockdiag(jnp.swapaxes(z_re, 2, 3), None))
    v_im_b = _split_stack(blockdiag(jnp.swapaxes(-z_im, 2, 3), None))
    k_b = _split_stack(blockdiag(jnp.swapaxes(k_tap, 2, 3), None))
    al_re = pw_re[chunk].reshape(nblk, 1, gpb * p)
    al_im = pw_im[chunk].reshape(nblk, 1, gpb * p)
    return w_re_b, w_im_b, v_re_b, v_im_b, k_b, al_re, al_im


def _s5_kernel(chunk, scan, u_ref, h0r_ref, h0i_ref, wre_ref, wim_ref, vre_ref, vim_ref, k_ref, ar_ref, ai_ref,
               d_ref, y_ref, hr_out, hi_out, hr_s, hi_s, sr_s, si_s):
    rc = sr_s.shape[0]
    xs = [u_ref[pl.ds(tok, rc, stride=chunk), :] for tok in range(chunk)]
    xb = [_split(x) for x in xs]
    sre = _dot3(xb[0], _pair(wre_ref, 0, 0))
    sim = _dot3(xb[0], _pair(wim_ref, 0, 0))
    for tok in range(1, chunk):
        sre = sre + _dot3(xb[tok], _pair(wre_ref, 0, tok))
        sim = sim + _dot3(xb[tok], _pair(wim_ref, 0, tok))
    ar, ai = ar_ref[0], ai_ref[0]
    if scan:
        @pl.when(pl.program_id(2) == 0)
        def _():
            hr_s[...] = h0r_ref[0, 0]
            hi_s[...] = h0i_ref[0, 0]

        sr_s[...] = sre
        si_s[...] = sim

        def body(c, carry):
            hr, hi = carry
            sr = sr_s[pl.ds(c, 1), :]
            si = si_s[pl.ds(c, 1), :]
            sr_s[pl.ds(c, 1), :] = hr
            si_s[pl.ds(c, 1), :] = hi
            return ar * hr - ai * hi + sr, ar * hi + ai * hr + si

        hr, hi = lax.fori_loop(0, rc, body, (hr_s[...], hi_s[...]))
        hr_s[...] = hr
        hi_s[...] = hi
        hr_out[0, 0] = hr
        hi_out[0, 0] = hi
        h_re, h_im = sr_s[...], si_s[...]
    else:
        h_re, h_im = h0r_ref[0], h0i_ref[0]
        hr_out[0] = ar * h_re - ai * h_im + sre
        hi_out[0] = ar * h_im + ai * h_re + sim
    hb_re, hb_im = _split(h_re), _split(h_im)
    for tok in range(chunk):
        acc = _dot3(hb_re, _pair(vre_ref, 0, tok)) + _dot3(hb_im, _pair(vim_ref, 0, tok))
        for s in range(tok + 1):
            acc = acc + _dot3(xb[s], _pair(k_ref, 0, tok - s))
        acc = acc + d_ref[0] * xs[tok]
        y_ref[pl.ds(tok, rc, stride=chunk), :] = acc


def s5_prompt(proj, h0r, h0i, wts, d_skip, bsz, t, chunk, rc):
    w_re, w_im, v_re, v_im, k_b, al_re, al_im = wts
    n = proj.shape[0]
    nblk = w_re.shape[1]
    sw = w_re.shape[4]
    cw = nblk * LANES
    nt = t // (rc * chunk)
    wspec = lambda a: pl.BlockSpec((2, 1) + a.shape[2:], lambda k, b, j: (0, k) + (0,) * (a.ndim - 2))
    aspec = lambda a: pl.BlockSpec((1,) + a.shape[1:], lambda k, b, j: (k,) + (0,) * (a.ndim - 1))
    hspec = pl.BlockSpec((1, 1, 1, sw), lambda k, b, j: (k, b, 0, 0))
    return pl.pallas_call(
        functools.partial(_s5_kernel, chunk, True),
        grid=(nblk, bsz, nt),
        in_specs=[pl.BlockSpec((rc * chunk, LANES), lambda k, b, j: (b * nt + j, k)),
                  hspec, hspec,
                  wspec(w_re), wspec(w_im), wspec(v_re), wspec(v_im), wspec(k_b), aspec(al_re), aspec(al_im),
                  pl.BlockSpec((1, 1, LANES), lambda k, b, j: (k, 0, 0))],
        out_specs=[pl.BlockSpec((rc * chunk, LANES), lambda k, b, j: (b * nt + j, k)), hspec, hspec],
        out_shape=[jax.ShapeDtypeStruct((n, cw), F32),
                   jax.ShapeDtypeStruct((nblk, bsz, 1, sw), F32),
                   jax.ShapeDtypeStruct((nblk, bsz, 1, sw), F32)],
        scratch_shapes=[pltpu.VMEM((1, sw), F32), pltpu.VMEM((1, sw), F32),
                        pltpu.VMEM((rc, sw), F32), pltpu.VMEM((rc, sw), F32)],
        compiler_params=_params("parallel", "parallel", "arbitrary"),
    )(proj, h0r, h0i, w_re, w_im, v_re, v_im, k_b, al_re, al_im, d_skip.reshape(nblk, 1, LANES))


def s5_sample(proj, h0r, h0i, wts, d_skip, bsz, chunk):
    w_re, w_im, v_re, v_im, k_b, al_re, al_im = wts
    n = proj.shape[0]
    nblk = w_re.shape[1]
    sw = w_re.shape[4]
    cw = nblk * LANES
    wspec = lambda a: pl.BlockSpec((2, 1) + a.shape[2:], lambda k: (0, k) + (0,) * (a.ndim - 2))
    aspec = lambda a: pl.BlockSpec((1,) + a.shape[1:], lambda k: (k,) + (0,) * (a.ndim - 1))
    hspec = pl.BlockSpec((1, bsz, sw), lambda k: (k, 0, 0))
    return pl.pallas_call(
        functools.partial(_s5_kernel, chunk, False),
        grid=(nblk,),
        in_specs=[pl.BlockSpec((n, LANES), lambda k: (0, k)),
                  hspec, hspec,
                  wspec(w_re), wspec(w_im), wspec(v_re), wspec(v_im), wspec(k_b), aspec(al_re), aspec(al_im),
                  pl.BlockSpec((1, 1, LANES), lambda k: (k, 0, 0))],
        out_specs=[pl.BlockSpec((n, LANES), lambda k: (0, k)), hspec, hspec],
        out_shape=[jax.ShapeDtypeStruct((n, cw), F32),
                   jax.ShapeDtypeStruct((nblk, bsz, sw), F32),
                   jax.ShapeDtypeStruct((nblk, bsz, sw), F32)],
        scratch_shapes=[pltpu.VMEM((1, sw), F32), pltpu.VMEM((1, sw), F32),
                        pltpu.VMEM((bsz, sw), F32), pltpu.VMEM((bsz, sw), F32)],
        compiler_params=_params("parallel"),
    )(proj, h0r, h0i, w_re, w_im, v_re, v_im, k_b, al_re, al_im, d_skip.reshape(nblk, 1, LANES))


def _glu_kernel(y_ref, w_ref, b_ref, o_ref):
    z = jax.nn.gelu(y_ref[...], approximate=True)
    o_ref[...] = z * jax.nn.sigmoid(_dot3(z, _pair(w_ref)) + b_ref[...])


def glu(y, w, b, tm):
    n, cw = y.shape
    return pl.pallas_call(
        _glu_kernel,
        grid=(n // tm,),
        in_specs=[pl.BlockSpec((tm, cw), lambda i: (i, 0)),
                  pl.BlockSpec((2, cw, cw), lambda i: (0, 0, 0)),
                  pl.BlockSpec((1, cw), lambda i: (0, 0))],
        out_specs=pl.BlockSpec((tm, cw), lambda i: (i, 0)),
        out_shape=jax.ShapeDtypeStruct((n, cw), F32),
        compiler_params=_params("parallel"),
    )(y, _split_stack(w), b.reshape(1, cw))


def _moba_prompt_kernel(nb, q_ref, k_ref, v_ref, o_ref, kmean_ref):
    i = pl.program_id(2)
    tq = q_ref.shape[0]

    @pl.when(i == 0)
    def _():
        kmean_ref[...] = jnp.zeros_like(kmean_ref)
        for n in range(nb):
            kmean_ref[n:n + 1, :] = jnp.mean(k_ref[n * MOBA_BLOCK:(n + 1) * MOBA_BLOCK, :], axis=0, keepdims=True)

    qb = _split(q_ref[...])
    gate = _dot3_nt(qb, kmean_ref[...])
    blk = lax.broadcasted_iota(I32, gate.shape, 1)
    gate = jnp.where(blk < i, gate, -jnp.inf)
    sel = jnp.zeros(gate.shape, F32)
    for _ in range(MOBA_TOPK):
        top = jnp.max(gate, axis=-1, keepdims=True)
        first = jnp.min(jnp.where(gate == top, blk, nb + LANES), axis=-1, keepdims=True)
        pick = (blk == first) & (first < i)
        sel = jnp.where(pick, 1.0, sel)
        gate = jnp.where(blk == first, -jnp.inf, gate)

    tk = 2 * MOBA_BLOCK
    row = lax.broadcasted_iota(I32, (tq, tk), 0)
    col = lax.broadcasted_iota(I32, (tq, tk), 1)
    second = col >= MOBA_BLOCK
    causal = jnp.where(jnp.where(second, col - MOBA_BLOCK, col) <= row, 1.0, 0.0)

    def att_block(c, carry):
        m, l, acc = carry
        off = pl.multiple_of(c * tk, tk)
        s = _dot3_nt(qb, k_ref[pl.ds(off, tk), :]) * HEAD_DIM ** -0.5
        picked0 = jnp.max(jnp.where(blk == 2 * c, sel, 0.0), axis=-1, keepdims=True)
        picked1 = jnp.max(jnp.where(blk == 2 * c + 1, sel, 0.0), axis=-1, keepdims=True)
        own = jnp.where(second, 2 * c + 1, 2 * c) == i
        valid = jnp.where(own, causal, jnp.where(second, picked1, picked0)) > 0.0
        s = jnp.where(valid, s, NEG_BIG)
        m_new = jnp.maximum(m, jnp.max(s, axis=-1, keepdims=True))
        alpha = jnp.exp(m - m_new)
        p = jnp.where(valid, jnp.exp(s - m_new), 0.0)
        l = alpha * l + jnp.sum(p, axis=-1, keepdims=True)
        acc = alpha * acc + _dot3(p, v_ref[pl.ds(off, tk), :])
        return m_new, l, acc

    m, l, acc = lax.fori_loop(
        0, i // 2 + 1, att_block,
        (jnp.full((tq, 1), NEG_BIG, F32), jnp.zeros((tq, 1), F32), jnp.zeros((tq, HEAD_DIM), F32)))
    o_ref[...] = acc / l


def moba_prompt(proj, bsz, t):
    n = proj.shape[0]
    n_heads = proj.shape[1] // 4 // HEAD_DIM
    nb = t // MOBA_BLOCK
    assert t % (2 * MOBA_BLOCK) == 0
    nbp = -(-nb // 8) * 8
    return pl.pallas_call(
        functools.partial(_moba_prompt_kernel, nb),
        grid=(bsz, n_heads, nb),
        in_specs=[pl.BlockSpec((MOBA_BLOCK, HEAD_DIM), lambda b, h, i: (b * nb + i, n_heads + h)),
                  pl.BlockSpec((t, HEAD_DIM), lambda b, h, i: (b, 2 * n_heads + h)),
                  pl.BlockSpec((t, HEAD_DIM), lambda b, h, i: (b, 3 * n_heads + h))],
        out_specs=pl.BlockSpec((MOBA_BLOCK, HEAD_DIM), lambda b, h, i: (b * nb + i, h)),
        out_shape=jax.ShapeDtypeStruct((n, n_heads * HEAD_DIM), F32),
        scratch_shapes=[pltpu.VMEM((nbp, HEAD_DIM), F32)],
        compiler_params=_params("parallel", "parallel", "arbitrary"),
    )(proj, proj, proj)


def _router_kernel(x_ref, g_ref, sc_ref, sh_ref, w_ref, b_ref, hn_ref, eid_ref, rank_ref, wt_ref, cnt_ref, carry_ref):
    i = pl.program_id(0)
    tm = x_ref.shape[0]

    @pl.when(i == 0)
    def _():
        carry_ref[...] = jnp.zeros_like(carry_ref)

    hn = _norm_mod(x_ref[...], g_ref[...], sc_ref[0, 0], sh_ref[0, 0])
    hn_ref[...] = hn
    logit = _dot3(hn, w_ref[...]) + b_ref[...]
    lane = lax.broadcasted_iota(I32, logit.shape, 1)
    gmask = lane < N_GROUPS
    gl = jnp.where(gmask, logit, -jnp.inf)
    gmax = jnp.max(gl, axis=-1, keepdims=True)
    gsel = jnp.min(jnp.where(gl == gmax, lane, LANES), axis=-1, keepdims=True)
    g_gate = 1.0 / jnp.sum(jnp.where(gmask, jnp.exp(gl - gmax), 0.0), axis=-1, keepdims=True)
    lo = N_GROUPS + gsel * EXP_PER_GROUP
    emask = (lane >= lo) & (lane < lo + EXP_PER_GROUP)
    el = jnp.where(emask, logit, -jnp.inf)
    v1 = jnp.max(el, axis=-1, keepdims=True)
    i1 = jnp.min(jnp.where(el == v1, lane, LANES), axis=-1, keepdims=True)
    el2 = jnp.where(lane == i1, -jnp.inf, el)
    v2 = jnp.max(el2, axis=-1, keepdims=True)
    i2 = jnp.min(jnp.where(el2 == v2, lane, LANES), axis=-1, keepdims=True)
    e21 = jnp.exp(v2 - v1)
    w1 = g_gate / (1.0 + e21)
    w2 = w1 * e21
    oh1 = jnp.where(lane == i1, 1.0, 0.0)
    oh2 = jnp.where(lane == i2, 1.0, 0.0)
    oh = oh1 + oh2
    r_i = lax.broadcasted_iota(I32, (tm, tm), 0)
    c_i = lax.broadcasted_iota(I32, (tm, tm), 1)
    tri = jnp.where(c_i < r_i, 1.0, 0.0).astype(BF16)
    before = _dot(tri, oh.astype(BF16)) + carry_ref[0:1, :]
    rank1 = jnp.sum(oh1 * before, axis=-1, keepdims=True).astype(I32)
    rank2 = jnp.sum(oh2 * before, axis=-1, keepdims=True).astype(I32)
    eid_ref[...] = jnp.where(lane == 0, i1 - N_GROUPS, jnp.where(lane == 1, i2 - N_GROUPS, 0))
    rank_ref[...] = jnp.where(lane == 0, rank1, jnp.where(lane == 1, rank2, 0))
    wt_ref[...] = jnp.where(lane == 0, w1, jnp.where(lane == 1, w2, 0.0))
    carry_ref[0:1, :] = carry_ref[0:1, :] + jnp.sum(oh, axis=0, keepdims=True)
    cnt_ref[...] = carry_ref[...]


def router(x, g, mod6, w_r, b_r, tm, rows_per_mod):
    n, d = x.shape
    rm = mod6.shape[2]
    tok = lambda width, dt: (pl.BlockSpec((tm, width), lambda i: (i, 0)), jax.ShapeDtypeStruct((n, width), dt))
    specs = [tok(d, F32), tok(LANES, I32), tok(LANES, I32), tok(LANES, F32),
             (pl.BlockSpec((8, LANES), lambda i: (0, 0)), jax.ShapeDtypeStruct((8, LANES), F32))]
    return pl.pallas_call(
        _router_kernel,
        grid=(n // tm,),
        in_specs=[pl.BlockSpec((tm, d), lambda i: (i, 0)),
                  pl.BlockSpec((1, d), lambda i: (0, 0)),
                  _mod_spec(4, tm, rows_per_mod, rm, d, 1),
                  _mod_spec(3, tm, rows_per_mod, rm, d, 1),
                  pl.BlockSpec((d, LANES), lambda i: (0, 0)),
                  pl.BlockSpec((1, LANES), lambda i: (0, 0))],
        out_specs=[s for s, _ in specs],
        out_shape=[o for _, o in specs],
        scratch_shapes=[pltpu.VMEM((8, LANES), F32)],
        compiler_params=_params("arbitrary"),
    )(x, g.reshape(1, d), mod6, mod6, w_r, b_r)


def _expert_kernel(te_ref, tv_ref, src_ref, dst_ref, hn_hbm, w1_ref, w3_ref, w2_ref, y_hbm, xbuf, ybuf, gsem, ssem):
    j = pl.program_id(0)
    nj = pl.num_programs(0)
    tm = ybuf.shape[0]

    def gather_rows(tile, op):
        slot = tile % 2

        def body(r, c):
            op(pltpu.make_async_copy(hn_hbm.at[pl.ds(src_ref[tile * tm + r], 1)], xbuf.at[slot, pl.ds(r, 1)],
                                     gsem.at[slot]))
            return c
        lax.fori_loop(0, tm, body, 0, unroll=8)

    def scatter_rows(tile, op):
        def body(r, c):
            row = dst_ref[tile * tm + r]

            @pl.when(row >= 0)
            def _():
                op(pltpu.make_async_copy(ybuf.at[pl.ds(r, 1)], y_hbm.at[pl.ds(row, 1)], ssem))
            return c
        lax.fori_loop(0, tm, body, 0, unroll=8)

    start = lambda c: c.start()
    wait = lambda c: c.wait()

    @pl.when((j == 0) & (tv_ref[0] > 0))
    def _():
        gather_rows(0, start)

    nxt = jnp.minimum(j + 1, nj - 1)

    @pl.when((j + 1 < nj) & (tv_ref[nxt] > 0))
    def _():
        gather_rows(j + 1, start)

    prev = jnp.maximum(j - 1, 0)

    @pl.when((j > 0) & (tv_ref[prev] > 0))
    def _():
        scatter_rows(j - 1, wait)

    @pl.when(tv_ref[j] > 0)
    def _():
        gather_rows(j, wait)
        xb = _split(xbuf[j % 2])
        a = _dot3(xb, w1_ref[0, 0])
        b = _dot3(xb, w3_ref[0, 0])
        h = (a * jax.nn.sigmoid(a)) * b
        ybuf[...] = _dot3(h, w2_ref[0, 0])
        scatter_rows(j, start)

        @pl.when(j == nj - 1)
        def _():
            scatter_rows(j, wait)


def experts(hn, tile_e, tile_v, src, dst, w1, w3, w2, layer, tm, n_out):
    n_tiles = tile_e.shape[0]
    d = hn.shape[1]
    ff = w1.shape[3]
    grid_spec = pltpu.PrefetchScalarGridSpec(
        num_scalar_prefetch=4,
        grid=(n_tiles,),
        in_specs=[pl.BlockSpec(memory_space=pl.ANY),
                  pl.BlockSpec((1, 1, d, ff), lambda j, te, tv, s, t: (layer, te[j], 0, 0)),
                  pl.BlockSpec((1, 1, d, ff), lambda j, te, tv, s, t: (layer, te[j], 0, 0)),
                  pl.BlockSpec((1, 1, ff, d), lambda j, te, tv, s, t: (layer, te[j], 0, 0))],
        out_specs=pl.BlockSpec(memory_space=pl.ANY),
        scratch_shapes=[pltpu.VMEM((2, tm, d), F32), pltpu.VMEM((tm, d), F32),
                        pltpu.SemaphoreType.DMA((2,)), pltpu.SemaphoreType.DMA(())],
    )
    return pl.pallas_call(
        _expert_kernel,
        grid_spec=grid_spec,
        out_shape=jax.ShapeDtypeStruct((n_out, d), F32),
        compiler_params=_params("arbitrary"),
    )(tile_e, tile_v, src, dst, hn, w1, w3, w2)


def _combine_kernel(x_ref, g_ref, wt_ref, y0_ref, y1_ref, o_ref):
    wt = wt_ref[...]
    o_ref[...] = x_ref[...] + g_ref[0, 0] * (wt[:, 0:1] * y0_ref[...] + wt[:, 1:2] * y1_ref[...])


def combine(x, mod6, wt, y, tm, rows_per_mod):
    n, d = x.shape
    rm = mod6.shape[2]
    nt = n // tm
    return pl.pallas_call(
        _combine_kernel,
        grid=(nt,),
        in_specs=[pl.BlockSpec((tm, d), lambda i: (i, 0)),
                  _mod_spec(5, tm, rows_per_mod, rm, d, 1),
                  pl.BlockSpec((tm, LANES), lambda i: (i, 0)),
                  pl.BlockSpec((tm, d), lambda i: (i, 0)),
                  pl.BlockSpec((tm, d), lambda i: (i + nt, 0))],
        out_specs=pl.BlockSpec((tm, d), lambda i: (i, 0)),
        out_shape=jax.ShapeDtypeStruct((n, d), F32),
        compiler_params=_params("parallel"),
    )(x, mod6, wt, y, y)


def hier_moe(x, g, mod6, w_r, b_r, w1, w3, w2, layer, tm, rows_per_mod, tm_e):
    n, d = x.shape
    hn, eid, rank, wt, cnt = router(x, g, mod6, w_r, b_r, tm, rows_per_mod)
    cnt = cnt[0, N_GROUPS:N_GROUPS + N_EXPERTS].astype(I32)
    padded = ((cnt + tm_e - 1) // tm_e) * tm_e
    ends = jnp.cumsum(padded)
    offs = ends - padded
    e2, r2 = eid[:, :MOE_TOPK], rank[:, :MOE_TOPK]
    pos = (offs[e2] + r2).reshape(-1)
    n_tiles = (n * MOE_TOPK) // tm_e + N_EXPERTS
    n_rows = n_tiles * tm_e
    tok_id = jnp.repeat(jnp.arange(n, dtype=I32), MOE_TOPK)
    pick_id = jnp.tile(jnp.arange(MOE_TOPK, dtype=I32), n)
    dst = jnp.full((n_rows,), -1, I32).at[pos].set(pick_id * n + tok_id)
    src = jnp.where(dst >= 0, dst % n, 0)
    starts = jnp.arange(n_tiles, dtype=I32) * tm_e
    tile_e = jnp.minimum(jnp.sum((starts[:, None] >= ends[None, :]).astype(I32), axis=1), N_EXPERTS - 1)
    tile_v = (starts < ends[-1]).astype(I32)
    n_out = n * MOE_TOPK
    y = experts(hn, tile_e, tile_v, src, dst, w1, w3, w2, layer, tm_e, n_out)
    return combine(x, mod6, wt, y, tm, rows_per_mod)


SAMPLE_ROWS = 8


def _dsa_select_kernel(layer, n_pages, n_sel, t_new, pt_ref, iq_ref, iw_ref, ikn_ref, kidx_hbm,
                       idx_ref, cnt_ref, mnew_ref, kbuf, score_ref, key_ref, pfx_ref, sem):
    b = pl.program_id(0)
    n_past = n_pages * PAGE_SIZE
    tp = SAMPLE_ROWS
    ck = min(2048, n_past)

    def page_copy(pg):
        return pltpu.make_async_copy(kidx_hbm.at[layer, pt_ref[b * n_pages + pg]],
                                     kbuf.at[pl.ds(pg * PAGE_SIZE, PAGE_SIZE)], sem)

    def start(pg, c):
        page_copy(pg).start()
        return c

    def wait(pg, c):
        page_copy(pg).wait()
        return c

    lax.fori_loop(0, n_pages, start, 0)
    lax.fori_loop(0, n_pages, wait, 0)

    iq = _split(iq_ref[0])
    iw = iw_ref[0]

    def head_sum(s):
        s = jnp.maximum(s * IDX_DIM ** -0.5, 0.0) * iw
        rows = [jnp.sum(s[t * IDX_HEADS:(t + 1) * IDX_HEADS], axis=0, keepdims=True) for t in range(tp)]
        return jnp.concatenate(rows, axis=0) * IDX_HEADS ** -0.5

    def score_chunk(c, carry):
        off = pl.multiple_of(c * ck, ck)
        sc = head_sum(_dot3_nt(iq, kbuf[pl.ds(off, ck), :]))
        score_ref[:, pl.ds(off, ck)] = sc
        key_ref[:, pl.ds(off, ck)] = _float_key(sc)
        return carry

    lax.fori_loop(0, n_past // ck, score_chunk, 0)
    ikn = jnp.concatenate([ikn_ref[0], jnp.zeros((LANES - tp, IDX_DIM), F32)], axis=0)
    s_new = head_sum(_dot3_nt(iq, ikn))
    trow = lax.broadcasted_iota(I32, (tp, LANES), 0)
    jcol = lax.broadcasted_iota(I32, (tp, LANES), 1)
    s_new = jnp.where((jcol <= trow) & (jcol < t_new), s_new, -jnp.inf)
    key_new = _float_key(s_new)

    def count_ge(cand):
        def body(c, acc):
            off = pl.multiple_of(c * ck, ck)
            m = jnp.where(key_ref[:, pl.ds(off, ck)] >= cand, 1.0, 0.0)
            for q in range(ck // LANES):
                acc = acc + m[:, q * LANES:(q + 1) * LANES]
            return acc
        acc = lax.fori_loop(0, n_past // ck, body, jnp.where(key_new >= cand, 1.0, 0.0))
        return jnp.sum(acc, axis=-1, keepdims=True)

    def bit_step(t, cur):
        cand = cur + lax.shift_left(jnp.int32(1), 31 - t)
        return jnp.where(count_ge(cand) >= n_sel, cand, cur)

    thr = lax.fori_loop(0, 32, bit_step, jnp.full((tp, 1), INT_MIN, I32))
    mnew_ref[0] = jnp.where((key_new >= thr) & (s_new > -jnp.inf), 1.0, 0.0)

    pb = 256
    r_i = lax.broadcasted_iota(I32, (pb, pb), 0)
    c_i = lax.broadcasted_iota(I32, (pb, pb), 1)
    upper = jnp.where(r_i < c_i, 1.0, 0.0).astype(BF16)

    def prefix_block(c, run):
        off = pl.multiple_of(c * pb, pb)
        m = jnp.where(key_ref[:, pl.ds(off, pb)] >= thr, 1.0, 0.0)
        before = _dot(m.astype(BF16), upper) + run
        pfx_ref[:, pl.ds(off, pb)] = jnp.where(m > 0.0, before, -1.0)
        return run + jnp.sum(m, axis=-1, keepdims=True)

    total = lax.fori_loop(0, n_past // pb, prefix_block, jnp.zeros((tp, 1), F32))
    cnt_ref[0] = jnp.broadcast_to(jnp.minimum(total, float(n_sel)), (tp, LANES))

    slot = lax.broadcasted_iota(I32, (n_sel, ck), 0).astype(F32)
    kpos = lax.broadcasted_iota(I32, (n_sel, ck), 1).astype(F32)
    for t in range(t_new):
        def slot_chunk(c, acc, t=t):
            off = pl.multiple_of(c * ck, ck)
            pf = pfx_ref[t:t + 1, pl.ds(off, ck)]
            hit = jnp.where(pf == slot, kpos + jnp.asarray(c * ck, F32), 0.0)
            return acc + jnp.sum(hit, axis=-1, keepdims=True)
        pos = lax.fori_loop(0, n_past // ck, slot_chunk, jnp.zeros((n_sel, 1), F32))
        idx_ref[0, t] = pos.astype(I32)


def dsa_select(iq, iw, ik_new, pool_kidx, page_table, layer, t_new):
    bsz = iq.shape[0]
    n_pages = page_table.shape[1]
    n_past = n_pages * PAGE_SIZE
    n_sel = min(IDX_TOPK, (n_past + t_new) // 4)
    tp = SAMPLE_ROWS
    grid_spec = pltpu.PrefetchScalarGridSpec(
        num_scalar_prefetch=1,
        grid=(bsz,),
        in_specs=[pl.BlockSpec((1, tp * IDX_HEADS, IDX_DIM), lambda b, pt: (b, 0, 0)),
                  pl.BlockSpec((1, tp * IDX_HEADS, 1), lambda b, pt: (b, 0, 0)),
                  pl.BlockSpec((1, tp, IDX_DIM), lambda b, pt: (b, 0, 0)),
                  pl.BlockSpec(memory_space=pl.ANY)],
        out_specs=[pl.BlockSpec((1, t_new, n_sel, 1), lambda b, pt: (b, 0, 0, 0)),
                   pl.BlockSpec((1, tp, LANES), lambda b, pt: (b, 0, 0)),
                   pl.BlockSpec((1, tp, LANES), lambda b, pt: (b, 0, 0))],
        scratch_shapes=[pltpu.VMEM((n_past, IDX_DIM), F32), pltpu.VMEM((tp, n_past), F32),
                        pltpu.VMEM((tp, n_past), I32), pltpu.VMEM((tp, n_past), F32),
                        pltpu.SemaphoreType.DMA(())],
    )
    return pl.pallas_call(
        functools.partial(_dsa_select_kernel, layer, n_pages, n_sel, t_new),
        grid_spec=grid_spec,
        out_shape=[jax.ShapeDtypeStruct((bsz, t_new, n_sel, 1), I32),
                   jax.ShapeDtypeStruct((bsz, tp, LANES), F32),
                   jax.ShapeDtypeStruct((bsz, tp, LANES), F32)],
        compiler_params=_params("arbitrary"),
    )(page_table.reshape(-1), iq, iw, ik_new, pool_kidx)


def _dsa_gather_kernel(layer, n_pages, n_sel, t_new, pt_ref, idx_ref, cnt_ref, mnew_ref, q_ref, kn_ref, vn_ref,
                       k_hbm, v_hbm, o_ref, kbuf, vbuf, ksem, vsem):
    b = pl.program_id(0)
    t = pl.program_id(1)
    base = (b * t_new + t) * n_sel

    def copies(j):
        pos = idx_ref[base + j]
        page = pt_ref[b * n_pages + pos // PAGE_SIZE]
        row = pos % PAGE_SIZE
        return (pltpu.make_async_copy(k_hbm.at[layer, page, pl.ds(row, 1)], kbuf.at[pl.ds(j, 1)], ksem),
                pltpu.make_async_copy(v_hbm.at[layer, page, pl.ds(row, 1)], vbuf.at[pl.ds(j, 1)], vsem))

    def start(j, c):
        ck, cv = copies(j)
        ck.start()
        cv.start()
        return c

    def wait(j, c):
        ck, cv = copies(j)
        ck.wait()
        cv.wait()
        return c

    lax.fori_loop(0, n_sel, start, 0)
    lax.fori_loop(0, n_sel, wait, 0)

    q = q_ref[0]
    scale = HEAD_DIM ** -0.5
    s = jnp.sum(kbuf[...] * q, axis=-1, keepdims=True) * scale
    slot = lax.broadcasted_iota(I32, s.shape, 0)
    s = jnp.where(slot < cnt_ref[b * t_new + t], s, NEG_BIG)
    kn = kn_ref[0]
    s_n = jnp.sum(kn * q, axis=-1, keepdims=True) * scale
    tn = lax.broadcasted_iota(I32, s_n.shape, 0)
    ok = jnp.zeros(s_n.shape, F32)
    for j in range(t_new):
        ok = jnp.where(tn == j, mnew_ref[(b * t_new + t) * t_new + j].astype(F32), ok)
    s_n = jnp.where(ok > 0.0, s_n, NEG_BIG)
    m = jnp.maximum(jnp.max(s, axis=0, keepdims=True), jnp.max(s_n, axis=0, keepdims=True))
    p = jnp.exp(s - m)
    p_n = jnp.exp(s_n - m)
    l = jnp.sum(p, axis=0, keepdims=True) + jnp.sum(p_n, axis=0, keepdims=True)
    acc = jnp.sum(p * vbuf[...], axis=0, keepdims=True) + jnp.sum(p_n * vn_ref[0], axis=0, keepdims=True)
    o_ref[0] = acc / l


def dsa_gather(q, k_new, v_new, idx, cnt, mnew, pool_k, pool_v, page_table, layer):
    bsz, t_new, n_heads, dh = q.shape
    n_pages = page_table.shape[1]
    n_sel = idx.shape[0] // (bsz * t_new)
    grid_spec = pltpu.PrefetchScalarGridSpec(
        num_scalar_prefetch=4,
        grid=(bsz, t_new),
        in_specs=[pl.BlockSpec((1, 1, n_heads, dh), lambda b, t, *_: (b, t, 0, 0)),
                  pl.BlockSpec((1, t_new, n_heads, dh), lambda b, t, *_: (b, 0, 0, 0)),
                  pl.BlockSpec((1, t_new, n_heads, dh), lambda b, t, *_: (b, 0, 0, 0)),
                  pl.BlockSpec(memory_space=pl.ANY), pl.BlockSpec(memory_space=pl.ANY)],
        out_specs=pl.BlockSpec((1, 1, n_heads, dh), lambda b, t, *_: (b, t, 0, 0)),
        scratch_shapes=[pltpu.VMEM((n_sel, n_heads, dh), F32), pltpu.VMEM((n_sel, n_heads, dh), F32),
                        pltpu.SemaphoreType.DMA(()), pltpu.SemaphoreType.DMA(())],
    )
    return pl.pallas_call(
        functools.partial(_dsa_gather_kernel, layer, n_pages, n_sel, t_new),
        grid_spec=grid_spec,
        out_shape=jax.ShapeDtypeStruct((bsz, t_new, n_heads, dh), F32),
        compiler_params=_params("arbitrary", "arbitrary"),
    )(page_table.reshape(-1), idx, cnt, mnew, q, k_new, v_new, pool_k, pool_v)


def dsa_sample(proj, ikiw, pool_k, pool_v, pool_kidx, page_table, layer, bsz, t_new):
    aw = pool_k.shape[3] * pool_k.shape[4]
    n_heads = pool_k.shape[3]
    tp = SAMPLE_ROWS
    hs = (bsz, t_new, n_heads, HEAD_DIM)
    pad_t = lambda a: jnp.concatenate([a, jnp.zeros((bsz, tp - t_new) + a.shape[2:], a.dtype)], axis=1)
    iq = pad_t(proj[:, 3 * aw:3 * aw + IDX_HEADS * IDX_DIM].reshape(bsz, t_new, IDX_HEADS, IDX_DIM))
    iw = pad_t(ikiw[:, IDX_DIM:IDX_DIM + IDX_HEADS].reshape(bsz, t_new, IDX_HEADS))
    ik_new = pad_t(ikiw[:, :IDX_DIM].reshape(bsz, t_new, IDX_DIM))
    idx, cnt, mnew = dsa_select(iq.reshape(bsz, tp * IDX_HEADS, IDX_DIM), iw.reshape(bsz, tp * IDX_HEADS, 1),
                                ik_new, pool_kidx, page_table, layer, t_new)
    cnt = cnt[:, :t_new, 0].astype(I32).reshape(-1)
    mnew = mnew[:, :t_new, :t_new].astype(I32).reshape(-1)
    out = dsa_gather(proj[:, :aw].reshape(hs), proj[:, aw:2 * aw].reshape(hs), proj[:, 2 * aw:3 * aw].reshape(hs),
                     idx.reshape(-1), cnt, mnew, pool_k, pool_v, page_table, layer)
    return out.reshape(bsz * t_new, aw)


def _moba_mean_kernel(ppb, pt_ref, *refs):
    pages, o_ref = refs[:ppb], refs[ppb]
    acc = jnp.sum(pages[0][0, 0], axis=0)
    for pg in pages[1:]:
        acc = acc + jnp.sum(pg[0, 0], axis=0)
    o_ref[0, 0] = acc * (1.0 / MOBA_BLOCK)


def moba_block_means(pool_k, page_table, layer):
    bsz, n_pages = page_table.shape
    _, _, page, n_heads, dh = pool_k.shape
    ppb = MOBA_BLOCK // page
    nbp = n_pages // ppb

    def page_spec(j):
        return pl.BlockSpec((1, 1, page, n_heads, dh),
                            lambda b, n, pt: (layer, pt[b * n_pages + n * ppb + j], 0, 0, 0))

    grid_spec = pltpu.PrefetchScalarGridSpec(
        num_scalar_prefetch=1,
        grid=(bsz, nbp),
        in_specs=[page_spec(j) for j in range(ppb)],
        out_specs=pl.BlockSpec((1, 1, n_heads, dh), lambda b, n, pt: (b, n, 0, 0)),
    )
    return pl.pallas_call(
        functools.partial(_moba_mean_kernel, ppb),
        grid_spec=grid_spec,
        out_shape=jax.ShapeDtypeStruct((bsz, nbp, n_heads, dh), F32),
        compiler_params=_params("parallel", "arbitrary"),
    )(page_table.reshape(-1), *([pool_k] * ppb))


def _moba_pick_kernel(n_top, km_ref, q_ref, sel_ref):
    km = km_ref[0]
    nbp = km.shape[0]
    t_new = q_ref.shape[1]
    for t in range(t_new):
        s = jnp.sum(km * q_ref[0, t:t + 1], axis=-1, keepdims=True)
        blk = lax.broadcasted_iota(I32, s.shape, 0)
        for j in range(n_top):
            top = jnp.max(s, axis=0, keepdims=True)
            first = jnp.min(jnp.where(s == top, blk, nbp), axis=0, keepdims=True)
            sel_ref[0, t, j] = first[0]
            s = jnp.where(blk == first, -jnp.inf, s)


def moba_pick(kmean, q, n_top):
    bsz, nbp, n_heads, dh = kmean.shape
    t_new = q.shape[1]
    return pl.pallas_call(
        functools.partial(_moba_pick_kernel, n_top),
        grid=(bsz,),
        in_specs=[pl.BlockSpec((1, nbp, n_heads, dh), lambda b: (b, 0, 0, 0)),
                  pl.BlockSpec((1, t_new, n_heads, dh), lambda b: (b, 0, 0, 0))],
        out_specs=pl.BlockSpec((1, t_new, n_top, n_heads, 1), lambda b: (b, 0, 0, 0, 0)),
        out_shape=jax.ShapeDtypeStruct((bsz, t_new, n_top, n_heads, 1), I32),
        compiler_params=_params("parallel"),
    )(kmean, q)


def _moba_gather_kernel(layer, n_pages, n_top, ppb, pt_ref, sel_ref, q_ref, kn_ref, vn_ref, k_hbm, v_hbm, o_ref,
                        kbuf, vbuf, ksem, vsem):
    b = pl.program_id(0)
    t = pl.program_id(1)
    t_new = pl.num_programs(1)
    n_heads = kbuf.shape[0]
    page = PAGE_SIZE
    copies = []
    for h in range(n_heads):
        for j in range(n_top):
            blk = sel_ref[((b * t_new + t) * n_top + j) * n_heads + h]
            for pg in range(ppb):
                phys = pt_ref[b * n_pages + blk * ppb + pg]
                dst = pl.ds((j * ppb + pg) * page, page)
                copies.append(pltpu.make_async_copy(k_hbm.at[layer, phys, :, h, :], kbuf.at[h, dst], ksem))
                copies.append(pltpu.make_async_copy(v_hbm.at[layer, phys, :, h, :], vbuf.at[h, dst], vsem))
    for c in copies:
        c.start()
    for c in copies:
        c.wait()

    q = q_ref[0, 0]
    scale = HEAD_DIM ** -0.5
    s = jnp.sum(kbuf[...] * q, axis=-1, keepdims=True) * scale
    s_n = jnp.sum(kn_ref[0] * q, axis=-1, keepdims=True) * scale
    tn = lax.broadcasted_iota(I32, s_n.shape, 1)
    s_n = jnp.where(tn <= t, s_n, NEG_BIG)
    m = jnp.maximum(jnp.max(s, axis=1, keepdims=True), jnp.max(s_n, axis=1, keepdims=True))
    p = jnp.exp(s - m)
    p_n = jnp.exp(s_n - m)
    l = jnp.sum(p, axis=1, keepdims=True) + jnp.sum(p_n, axis=1, keepdims=True)
    acc = jnp.sum(p * vbuf[...], axis=1, keepdims=True) + jnp.sum(p_n * vn_ref[0], axis=1, keepdims=True)
    o_ref[0, 0] = acc / l


def moba_gather(q, k_new, v_new, sel, pool_k, pool_v, page_table, layer, n_top):
    bsz, t_new, n_heads, _, dh = q.shape
    n_pages = page_table.shape[1]
    ppb = MOBA_BLOCK // PAGE_SIZE
    grid_spec = pltpu.PrefetchScalarGridSpec(
        num_scalar_prefetch=2,
        grid=(bsz, t_new),
        in_specs=[pl.BlockSpec((1, 1, n_heads, 1, dh), lambda b, t, *_: (b, t, 0, 0, 0)),
                  pl.BlockSpec((1, n_heads, t_new, dh), lambda b, t, *_: (b, 0, 0, 0)),
                  pl.BlockSpec((1, n_heads, t_new, dh), lambda b, t, *_: (b, 0, 0, 0)),
                  pl.BlockSpec(memory_space=pl.ANY), pl.BlockSpec(memory_space=pl.ANY)],
        out_specs=pl.BlockSpec((1, 1, n_heads, 1, dh), lambda b, t, *_: (b, t, 0, 0, 0)),
        scratch_shapes=[pltpu.VMEM((n_heads, n_top * MOBA_BLOCK, dh), F32),
                        pltpu.VMEM((n_heads, n_top * MOBA_BLOCK, dh), F32),
                        pltpu.SemaphoreType.DMA(()), pltpu.SemaphoreType.DMA(())],
    )
    return pl.pallas_call(
        functools.partial(_moba_gather_kernel, layer, n_pages, n_top, ppb),
        grid_spec=grid_spec,
        out_shape=jax.ShapeDtypeStruct((bsz, t_new, n_heads, 1, dh), F32),
        compiler_params=_params("arbitrary", "arbitrary"),
    )(page_table.reshape(-1), sel, q, k_new, v_new, pool_k, pool_v)


def moba_sample(proj, pool_k, pool_v, page_table, layer, bsz, t_new):
    n_heads = pool_k.shape[3]
    aw = n_heads * HEAD_DIM
    n_past = page_table.shape[1] * PAGE_SIZE
    assert n_past % MOBA_BLOCK == 0 and t_new <= MOBA_BLOCK
    n_top = min(MOBA_TOPK, n_past // MOBA_BLOCK)
    hs = (bsz, t_new, n_heads, HEAD_DIM)
    q = proj[:, aw:2 * aw].reshape(hs)
    k_new = proj[:, 2 * aw:3 * aw].reshape(hs).transpose(0, 2, 1, 3)
    v_new = proj[:, 3 * aw:4 * aw].reshape(hs).transpose(0, 2, 1, 3)
    kmean = moba_block_means(pool_k, page_table, layer)
    sel = moba_pick(kmean, q, n_top)
    out = moba_gather(q[:, :, :, None, :], k_new, v_new, sel.reshape(-1), pool_k, pool_v, page_table, layer, n_top)
    return out.reshape(bsz * t_new, aw)


def _ab_weight(w):
    d = w.shape[0]
    aw = (w.shape[1] - IDX_HEADS * IDX_DIM - IDX_DIM - IDX_HEADS) // 6
    main = 3 * aw + IDX_HEADS * IDX_DIM
    small = IDX_DIM + IDX_HEADS
    pad = 512 - small
    return _split_stack(jnp.concatenate([w[:, :main], w[:, main + small:], w[:, main:main + small],
                                         jnp.zeros((d, pad), w.dtype)], axis=1))


def _router_weight(wg, bg, we, be):
    d = wg.shape[0]
    pad = LANES - N_GROUPS - N_EXPERTS
    w = jnp.concatenate([wg, we, jnp.zeros((d, pad), F32)], axis=1)
    b = jnp.concatenate([bg, be, jnp.zeros((pad,), F32)]).reshape(1, LANES)
    return w, b


def _trunk(x, mod, p, past, prompt):
    bsz, t, d = x.shape
    n = bsz * t
    depth = p['norm1'].shape[0]
    aw = d // 2
    n_heads = aw // HEAD_DIM
    if prompt:
        tm, rows_per_mod, tm_e = 512, t, 256
    else:
        tm, rows_per_mod, tm_e = n, n, 16
    xf = x.reshape(n, d)
    ab_states, cd_states = [], []
    pos = (0 if prompt else past['page_table'].shape[1] * PAGE_SIZE) + jnp.arange(t)
    for l in range(depth):
        i = l // 2
        m6 = mod[l].reshape(bsz, 6, d).transpose(1, 0, 2)
        if prompt:
            mod6 = m6[:, :, None, :]
        else:
            mod6 = jnp.repeat(m6, t, axis=1)[:, None, :, :]
        if l % 2 == 0:
            proj = norm_proj(xf, p['norm1'][l], mod6, 1, 0, _ab_weight(p['w_in_ab'][i]), tm, rows_per_mod)
            k = proj[:, aw:2 * aw]
            v = proj[:, 2 * aw:3 * aw]
            ikiw = proj[:, 7 * aw:7 * aw + LANES]
            ik = ikiw[:, :IDX_DIM]
            if prompt:
                zeros = jnp.zeros_like(ik)
                ik_e = _split_stack(jnp.concatenate([ik, zeros], axis=1))
                ik_o = _split_stack(jnp.concatenate([zeros, ik], axis=1))
                a = dsa_prompt(proj, ik_e, ik_o, n_heads, bsz, t, 256, 512)
                b, nbuf = conv_prompt(proj, p['conv_w'][i], bsz, t, 512)
                new_buf = nbuf[:, 8 - (CONV_W - 1):]
            else:
                a = dsa_sample(proj, ikiw, past['cache_a_k'], past['cache_a_v'], past['cache_a_kidx'],
                               past['page_table'], i, bsz, t)
                b, u = conv_sample(proj, p['conv_w'][i], past['state_b_conv'][i], t)
                new_buf = u.reshape(bsz, t, aw)[:, t - (CONV_W - 1):]
            xf = out_proj(a, b, p['w_out_ab'][i], xf, mod6, 2, tm, rows_per_mod)
            ab_states.append((k.reshape(bsz, t, n_heads, HEAD_DIM), v.reshape(bsz, t, n_heads, HEAD_DIM),
                              ik.reshape(bsz, t, IDX_DIM), new_buf))
        else:
            proj = norm_proj(xf, p['norm1'][l], mod6, 1, 0, _split_stack(p['w_in_cd'][i]), tm, rows_per_mod)
            k = proj[:, 2 * aw:3 * aw]
            v = proj[:, 3 * aw:4 * aw]
            chunk = 8 if prompt else t
            wts = s5_weights(p['s5_log_dt'][i], p['s5_a_re'][i], p['s5_a_im'][i], p['s5_b_re'][i], p['s5_b_im'][i],
                             p['s5_c_re'][i], p['s5_c_im'][i], chunk)
            nblk = aw // LANES
            if prompt:
                h0 = jnp.zeros((nblk, bsz, 1, (aw // S5_GROUP) * S5_STATE // nblk), F32)
                y, hr, hi = s5_prompt(proj, h0, h0, wts, p['s5_d'][i], bsz, t, chunk, 64)
                hr, hi = hr[:, :, 0], hi[:, :, 0]
                dd = moba_prompt(proj, bsz, t)
            else:
                to_blk = lambda s: s.reshape(bsz, nblk, -1).transpose(1, 0, 2)
                y, hr, hi = s5_sample(proj, to_blk(past['state_c_re'][i]), to_blk(past['state_c_im'][i]), wts,
                                      p['s5_d'][i], bsz, chunk)
                dd = moba_sample(proj, past['cache_d_k'], past['cache_d_v'], past['page_table'], i, bsz, t)
            from_blk = lambda s: s.transpose(1, 0, 2).reshape(bsz, aw // S5_GROUP, S5_STATE)
            c = glu(y, p['glu_w'][i], p['glu_b'][i], tm)
            xf = out_proj(c, dd, p['w_out_cd'][i], xf, mod6, 2, tm, rows_per_mod)
            cd_states.append((from_blk(hr), from_blk(hi), k.reshape(bsz, t, n_heads, HEAD_DIM),
                              v.reshape(bsz, t, n_heads, HEAD_DIM)))
        w_r, b_r = _router_weight(p['router_g_w'][l], p['router_g_b'][l], p['router_e_w'][l], p['router_e_b'][l])
        xf = hier_moe(xf, p['norm2'][l], mod6, w_r, b_r, p['exp_w1'], p['exp_w3'], p['exp_w2'], l, tm,
                      rows_per_mod, tm_e)
    y = final_norm(xf, p['final_norm'], tm).reshape(bsz, t, d)
    a_k, a_v, a_kidx, b_conv = [jnp.stack(s) for s in zip(*ab_states)]
    c_re, c_im, d_k, d_v = [jnp.stack(s) for s in zip(*cd_states)]
    return y, a_k, a_v, a_kidx, b_conv, c_re, c_im, d_k, d_v


def kernel(x_prompt, x_sample, cache_a_k, cache_a_v, cache_a_kidx, state_b_conv, state_c_re, state_c_im, cache_d_k, cache_d_v, page_table, c_prompt, c_sample, norm1, norm2, ada_w, ada_b, w_in_ab, conv_w, w_out_ab, w_in_cd, s5_a_re, s5_a_im, s5_b_re, s5_b_im, s5_c_re, s5_c_im, s5_d, s5_log_dt, glu_w, glu_b, w_out_cd, router_g_w, router_g_b, router_e_w, router_e_b, exp_w1, exp_w3, exp_w2, final_norm):
    p = dict(norm1=norm1, norm2=norm2, w_in_ab=w_in_ab, conv_w=conv_w, w_out_ab=w_out_ab, w_in_cd=w_in_cd,
             s5_a_re=s5_a_re, s5_a_im=s5_a_im, s5_b_re=s5_b_re, s5_b_im=s5_b_im, s5_c_re=s5_c_re, s5_c_im=s5_c_im,
             s5_d=s5_d, s5_log_dt=s5_log_dt, glu_w=glu_w, glu_b=glu_b, w_out_cd=w_out_cd, router_g_w=router_g_w,
             router_g_b=router_g_b, router_e_w=router_e_w, router_e_b=router_e_b, exp_w1=exp_w1, exp_w3=exp_w3,
             exp_w2=exp_w2, final_norm=final_norm)
    past = dict(cache_a_k=cache_a_k, cache_a_v=cache_a_v, cache_a_kidx=cache_a_kidx, state_b_conv=state_b_conv,
                state_c_re=state_c_re, state_c_im=state_c_im, cache_d_k=cache_d_k, cache_d_v=cache_d_v,
                page_table=page_table)
    bp, bs = c_prompt.shape[0], c_sample.shape[0]
    rows = -(-(bp + bs) // 8) * 8
    c_all = jnp.concatenate([c_prompt, c_sample, jnp.zeros((rows - bp - bs, c_prompt.shape[1]), F32)], axis=0)
    mod = ada_mod(c_all, ada_w, ada_b)
    outs_p = _trunk(x_prompt, mod[:, :bp], p, None, True)
    outs_s = _trunk(x_sample, mod[:, bp:bp + bs], p, past, False)
    return (outs_p[0], outs_s[0]) + tuple(outs_p[1:]) + tuple(outs_s[1:])
```

```python
import functools
from typing import NamedTuple

import jax
import jax.numpy as jnp
from jax import lax
from jax.experimental import pallas as pl
from jax.experimental.pallas import tpu as pltpu

F32 = jnp.float32
BF16 = jnp.bfloat16
I32 = jnp.int32

HEAD_DIM = 128
IDX_HEADS = 16
IDX_DIM = 64
IDX_TOPK = 256
CONV_W = 3
S5_GROUP = 16
S5_STATE = 64
MOBA_BLOCK = 256
MOBA_TOPK = 3
N_GROUPS = 4
EXP_PER_GROUP = 8
N_EXPERTS = N_GROUPS * EXP_PER_GROUP
MOE_TOPK = 2
PAGE_SIZE = 128
EPS = 1e-6

LANES = 128
NEG_BIG = -1e30
INT_MIN = -(2 ** 31)
VMEM_LIMIT = 56 * 1024 * 1024


def _params(*sem):
    return pltpu.CompilerParams(dimension_semantics=sem, vmem_limit_bytes=VMEM_LIMIT)


def _dot(a, b):
    return jnp.dot(a, b, preferred_element_type=F32)


def _dot_nt(a, b):
    return lax.dot_general(a, b, (((1,), (1,)), ((), ())), preferred_element_type=F32)


def _split(x):
    hi = x.astype(BF16)
    return hi, (x - hi.astype(F32)).astype(BF16)


def _split_stack(x):
    x = x.astype(F32)
    hi = lax.reduce_precision(x, exponent_bits=8, mantissa_bits=7)
    return jnp.stack([hi.astype(BF16), (x - hi).astype(BF16)])


def _dot3(a, b, dot=_dot):
    ah, al = a if isinstance(a, tuple) else _split(a)
    bh, bl = b if isinstance(b, tuple) else _split(b)
    return dot(ah, bh) + dot(ah, bl) + dot(al, bh)


def _dot3_nt(a, b):
    return _dot3(a, b, _dot_nt)


def _pair(ref, *idx):
    return ref[(0,) + idx], ref[(1,) + idx]


def _ada_kernel(c_ref, w_ref, b_ref, o_ref):
    c = c_ref[...]
    o_ref[0] = _dot3(c * jax.nn.sigmoid(c), w_ref[0]) + b_ref[0]


def ada_mod(c, ada_w, ada_b):
    depth, d, n6 = ada_w.shape
    r = c.shape[0]
    tn = 1024
    return pl.pallas_call(
        _ada_kernel,
        grid=(depth, n6 // tn),
        in_specs=[pl.BlockSpec((r, d), lambda l, j: (0, 0)),
                  pl.BlockSpec((1, d, tn), lambda l, j: (l, 0, j)),
                  pl.BlockSpec((1, 1, tn), lambda l, j: (l, 0, j))],
        out_specs=pl.BlockSpec((1, r, tn), lambda l, j: (l, 0, j)),
        out_shape=jax.ShapeDtypeStruct((depth, r, n6), F32),
        compiler_params=_params("parallel", "parallel"),
    )(c, ada_w, ada_b.reshape(depth, 1, n6))


def _norm_mod(x, g, sc, sh):
    y = x * lax.rsqrt(jnp.mean(x * x, axis=-1, keepdims=True) + EPS)
    return (y * g) * (1.0 + sc) + sh


def _norm_proj_kernel(x_ref, g_ref, sc_ref, sh_ref, w_ref, o_ref, hn_ref):
    @pl.when(pl.program_id(1) == 0)
    def _():
        hi, lo = _split(_norm_mod(x_ref[...], g_ref[...], sc_ref[0, 0], sh_ref[0, 0]))
        hn_ref[0] = hi
        hn_ref[1] = lo

    o_ref[...] = _dot3(_pair(hn_ref), _pair(w_ref))


def _mod_spec(which, tm, rows_per_mod, rm, d, nargs):
    if nargs == 2:
        return pl.BlockSpec((1, 1, rm, d), lambda i, j: (which, (i * tm) // rows_per_mod, 0, 0))
    return pl.BlockSpec((1, 1, rm, d), lambda i: (which, (i * tm) // rows_per_mod, 0, 0))


def norm_proj(x, g, mod6, sc_idx, sh_idx, w, tm, rows_per_mod):
    n, d = x.shape
    nc = w.shape[2]
    tn = 512
    rm = mod6.shape[2]
    return pl.pallas_call(
        _norm_proj_kernel,
        grid=(n // tm, nc // tn),
        in_specs=[pl.BlockSpec((tm, d), lambda i, j: (i, 0)),
                  pl.BlockSpec((1, d), lambda i, j: (0, 0)),
                  _mod_spec(sc_idx, tm, rows_per_mod, rm, d, 2),
                  _mod_spec(sh_idx, tm, rows_per_mod, rm, d, 2),
                  pl.BlockSpec((2, d, tn), lambda i, j: (0, 0, j))],
        out_specs=pl.BlockSpec((tm, tn), lambda i, j: (i, j)),
        out_shape=jax.ShapeDtypeStruct((n, nc), F32),
        scratch_shapes=[pltpu.VMEM((2, tm, d), BF16)],
        compiler_params=_params("parallel", "arbitrary"),
    )(x, g.reshape(1, d), mod6, mod6, w)


def _out_proj_kernel(a_ref, b_ref, wa_ref, wb_ref, x_ref, g_ref, o_ref):
    y = _dot3(a_ref[...], _pair(wa_ref)) + _dot3(b_ref[...], _pair(wb_ref))
    o_ref[...] = x_ref[...] + g_ref[0, 0] * y


def out_proj(a, b, w, x, mod6, g_idx, tm, rows_per_mod):
    n, d = x.shape
    ka, kb = a.shape[1], b.shape[1]
    rm = mod6.shape[2]
    tn = d // 2
    wa, wb = _split_stack(w[:ka]), _split_stack(w[ka:])
    return pl.pallas_call(
        _out_proj_kernel,
        grid=(n // tm, d // tn),
        in_specs=[pl.BlockSpec((tm, ka), lambda i, j: (i, 0)),
                  pl.BlockSpec((tm, kb), lambda i, j: (i, 0)),
                  pl.BlockSpec((2, ka, tn), lambda i, j: (0, 0, j)),
                  pl.BlockSpec((2, kb, tn), lambda i, j: (0, 0, j)),
                  pl.BlockSpec((tm, tn), lambda i, j: (i, j)),
                  pl.BlockSpec((1, 1, rm, tn), lambda i, j: (g_idx, (i * tm) // rows_per_mod, 0, j))],
        out_specs=pl.BlockSpec((tm, tn), lambda i, j: (i, j)),
        out_shape=jax.ShapeDtypeStruct((n, d), F32),
        compiler_params=_params("parallel", "parallel"),
    )(a, b, wa, wb, x, mod6)


def _final_norm_kernel(x_ref, g_ref, o_ref):
    x = x_ref[...]
    o_ref[...] = (x * lax.rsqrt(jnp.mean(x * x, axis=-1, keepdims=True) + EPS)) * g_ref[...]


def final_norm(x, g, tm):
    n, d = x.shape
    return pl.pallas_call(
        _final_norm_kernel,
        grid=(n // tm,),
        in_specs=[pl.BlockSpec((tm, d), lambda i: (i, 0)), pl.BlockSpec((1, d), lambda i: (0, 0))],
        out_specs=pl.BlockSpec((tm, d), lambda i: (i, 0)),
        out_shape=jax.ShapeDtypeStruct((n, d), F32),
        compiler_params=_params("parallel"),
    )(x, g.reshape(1, d))


def _conv_prompt_kernel(gb_ref, gc_ref, gh_ref, pc_ref, ph_ref, w_ref, o_ref, nb_ref):
    j = pl.program_id(1)
    tq = gc_ref.shape[0]
    u = gc_ref[...] * gh_ref[...]
    prev = pc_ref[...] * ph_ref[...]
    prev = jnp.where(j == 0, 0.0, prev)
    row = lax.broadcasted_iota(I32, u.shape, 0)
    u1 = jnp.where(row == 0, prev[7:8], pltpu.roll(u, 1, 0))
    u2 = jnp.where(row == 0, prev[6:7], jnp.where(row == 1, prev[7:8], pltpu.roll(u, 2, 0)))
    y = w_ref[0:1] * u2 + w_ref[1:2] * u1 + w_ref[2:3] * u
    o_ref[...] = (gb_ref[...] * y).astype(o_ref.dtype)
    nb_ref[0] = u[tq - 8:tq]


def conv_prompt(proj, conv_w, bsz, t, tq):
    n = proj.shape[0]
    w = conv_w.shape[1]
    nt = t // tq
    wpad = jnp.concatenate([conv_w, jnp.zeros((8 - CONV_W, w), F32)], axis=0)

    def halo(cb):
        return pl.BlockSpec((8, w), lambda b, j: (jnp.maximum((b * t + j * tq) // 8 - 1, 0), cb))

    return pl.pallas_call(
        _conv_prompt_kernel,
        grid=(bsz, nt),
        in_specs=[pl.BlockSpec((tq, w), lambda b, j: (b * nt + j, 4)),
                  pl.BlockSpec((tq, w), lambda b, j: (b * nt + j, 5)),
                  pl.BlockSpec((tq, w), lambda b, j: (b * nt + j, 6)),
                  halo(5), halo(6),
                  pl.BlockSpec((8, w), lambda b, j: (0, 0))],
        out_specs=[pl.BlockSpec((tq, w), lambda b, j: (b * nt + j, 0)),
                   pl.BlockSpec((1, 8, w), lambda b, j: (b, 0, 0))],
        out_shape=[jax.ShapeDtypeStruct((n, w), F32), jax.ShapeDtypeStruct((bsz, 8, w), F32)],
        compiler_params=_params("parallel", "arbitrary"),
    )(proj, proj, proj, proj, proj, wpad)


def _conv_sample_kernel(t_len, gb_ref, gc_ref, gh_ref, p0_ref, p1_ref, w_ref, o_ref, u_ref):
    u = gc_ref[...] * gh_ref[...]
    row = lax.broadcasted_iota(I32, u.shape, 0) % t_len
    u1 = jnp.where(row == 0, p1_ref[...], pltpu.roll(u, 1, 0))
    u2 = jnp.where(row == 0, p0_ref[...], jnp.where(row == 1, p1_ref[...], pltpu.roll(u, 2, 0)))
    y = w_ref[0:1] * u2 + w_ref[1:2] * u1 + w_ref[2:3] * u
    o_ref[...] = (gb_ref[...] * y).astype(o_ref.dtype)
    u_ref[...] = u


def conv_sample(proj, conv_w, buf, t_len):
    n = proj.shape[0]
    w = conv_w.shape[1]
    wpad = jnp.concatenate([conv_w, jnp.zeros((8 - CONV_W, w), F32)], axis=0)
    p0 = jnp.repeat(buf[:, 0], t_len, axis=0)
    p1 = jnp.repeat(buf[:, 1], t_len, axis=0)
    full = lambda cb: pl.BlockSpec((n, w), lambda i: (0, cb))
    return pl.pallas_call(
        functools.partial(_conv_sample_kernel, t_len),
        grid=(1,),
        in_specs=[full(4), full(5), full(6), full(0), full(0), pl.BlockSpec((8, w), lambda i: (0, 0))],
        out_specs=[full(0), full(0)],
        out_shape=[jax.ShapeDtypeStruct((n, w), F32), jax.ShapeDtypeStruct((n, w), F32)],
        compiler_params=_params("arbitrary"),
    )(proj, proj, proj, p0, p1, wpad)


def _float_key(s):
    b = pltpu.bitcast(s, I32)
    return b ^ (lax.shift_right_arithmetic(b, 31) & 0x7FFFFFFF)


def _dsa_prompt_kernel(n_sel, tk, q_ref, iq_ref, iw_ref, ike_ref, iko_ref, k_ref, v_ref, o_ref, key_ref, bias_ref):
    i = pl.program_id(1)
    tq = q_ref.shape[0]
    nck = ((i + 1) * tq + tk - 1) // tk

    @pl.when(pl.program_id(2) == 0)
    def _():
        _dsa_prompt_select(n_sel, tk, nck, i, iq_ref, iw_ref, ike_ref, iko_ref, key_ref, bias_ref)

    qh = _split(q_ref[...])

    def att_chunk(c, carry):
        m, l, acc = carry
        off = pl.multiple_of(c * tk, tk)
        s = _dot3_nt(qh, k_ref[pl.ds(off, tk), :]) * HEAD_DIM ** -0.5
        s = jnp.where(bias_ref[:, pl.ds(off, tk)] == 0.0, s, NEG_BIG)
        m_new = jnp.maximum(m, jnp.max(s, axis=-1, keepdims=True))
        alpha = jnp.exp(m - m_new)
        p = jnp.exp(s - m_new)
        l = alpha * l + jnp.sum(p, axis=-1, keepdims=True)
        acc = alpha * acc + _dot(p.astype(BF16), v_ref[pl.ds(off, tk), :].astype(BF16))
        return m_new, l, acc

    m, l, acc = lax.fori_loop(
        0, nck, att_chunk,
        (jnp.full((tq, 1), NEG_BIG, F32), jnp.zeros((tq, 1), F32), jnp.zeros((tq, HEAD_DIM), F32)))
    o_ref[...] = acc / l


def _dsa_prompt_select(n_sel, tk, nck, i, iq_ref, iw_ref, ike_ref, iko_ref, key_ref, bias_ref):
    tq = iq_ref.shape[0]
    row = i * tq + lax.broadcasted_iota(I32, (tq, tk), 0)
    col0 = lax.broadcasted_iota(I32, (tq, tk), 1)
    iw = iw_ref[...]

    def score_chunk(c, carry):
        off = pl.multiple_of(c * tk, tk)
        ke = _pair(ike_ref, pl.ds(off, tk))
        ko = _pair(iko_ref, pl.ds(off, tk))
        acc = jnp.zeros((tq, tk), F32)
        for hp in range(IDX_HEADS // 2):
            iq2 = _split(iq_ref[:, hp * LANES:(hp + 1) * LANES])
            s0 = _dot3_nt(iq2, ke) * IDX_DIM ** -0.5
            s1 = _dot3_nt(iq2, ko) * IDX_DIM ** -0.5
            acc = acc + jnp.maximum(s0, 0.0) * iw[:, IDX_DIM + 2 * hp:IDX_DIM + 2 * hp + 1]
            acc = acc + jnp.maximum(s1, 0.0) * iw[:, IDX_DIM + 2 * hp + 1:IDX_DIM + 2 * hp + 2]
        acc = acc * IDX_HEADS ** -0.5
        acc = jnp.where(off + col0 <= row, acc, -jnp.inf)
        key_ref[:, pl.ds(off, tk)] = _float_key(acc)
        return carry

    lax.fori_loop(0, nck, score_chunk, 0)

    def count_ge(cand):
        def body(c, acc):
            off = pl.multiple_of(c * tk, tk)
            m = jnp.where(key_ref[:, pl.ds(off, tk)] >= cand, 1.0, 0.0)
            for q in range(tk // LANES):
                acc = acc + m[:, q * LANES:(q + 1) * LANES]
            return acc
        acc = lax.fori_loop(0, nck, body, jnp.zeros((tq, LANES), F32))
        return jnp.sum(acc, axis=-1, keepdims=True)

    def bit_step(t, cur):
        cand = cur + lax.shift_left(jnp.int32(1), 31 - t)
        return jnp.where(count_ge(cand) >= n_sel, cand, cur)

    thr = lax.fori_loop(0, 32, bit_step, jnp.full((tq, 1), INT_MIN, I32))

    def bias_chunk(c, carry):
        off = pl.multiple_of(c * tk, tk)
        sel = (key_ref[:, pl.ds(off, tk)] >= thr) & (off + col0 <= row)
        bias_ref[:, pl.ds(off, tk)] = jnp.where(sel, 0.0, NEG_BIG)
        return carry

    lax.fori_loop(0, nck, bias_chunk, 0)


def dsa_prompt(proj, ik_e, ik_o, n_heads, bsz, t, tq, tk):
    n = proj.shape[0]
    nt = t // tq
    n_sel = min(IDX_TOPK, t // 4)
    return pl.pallas_call(
        functools.partial(_dsa_prompt_kernel, n_sel, tk),
        grid=(bsz, nt, n_heads),
        in_specs=[pl.BlockSpec((tq, HEAD_DIM), lambda b, i, h: (b * nt + i, h)),
                  pl.BlockSpec((tq, IDX_HEADS * IDX_DIM), lambda b, i, h: (b * nt + i, 3)),
                  pl.BlockSpec((tq, LANES), lambda b, i, h: (b * nt + i, 7 * n_heads)),
                  pl.BlockSpec((2, t, LANES), lambda b, i, h: (0, b, 0)),
                  pl.BlockSpec((2, t, LANES), lambda b, i, h: (0, b, 0)),
                  pl.BlockSpec((t, HEAD_DIM), lambda b, i, h: (b, n_heads + h)),
                  pl.BlockSpec((t, HEAD_DIM), lambda b, i, h: (b, 2 * n_heads + h))],
        out_specs=pl.BlockSpec((tq, HEAD_DIM), lambda b, i, h: (b * nt + i, h)),
        out_shape=jax.ShapeDtypeStruct((n, n_heads * HEAD_DIM), F32),
        scratch_shapes=[pltpu.VMEM((tq, t), I32), pltpu.VMEM((tq, t), F32)],
        compiler_params=_params("parallel", "arbitrary", "arbitrary"),
    )(proj, proj, proj, ik_e, ik_o, proj, proj)


def s5_weights(log_dt, a_re, a_im, b_re, b_im, c_re, c_im, chunk):
    hp = lax.Precision.HIGHEST
    g, p = a_re.shape
    dt = jnp.exp(log_dt.astype(F32))[:, None]
    lr, li = a_re.astype(F32), a_im.astype(F32)
    mag = jnp.exp(lr * dt)
    ab_re, ab_im = mag * jnp.cos(li * dt), mag * jnp.sin(li * dt)
    den = lr * lr + li * li
    f_re = ((ab_re - 1.0) * lr + ab_im * li) / den
    f_im = (ab_im * lr - (ab_re - 1.0) * li) / den
    br, bi = b_re.astype(F32), b_im.astype(F32)
    bb_re = f_re[..., None] * br - f_im[..., None] * bi
    bb_im = f_re[..., None] * bi + f_im[..., None] * br
    pr, pi = [jnp.ones_like(ab_re)], [jnp.zeros_like(ab_im)]
    for _ in range(chunk):
        pr.append(pr[-1] * ab_re - pi[-1] * ab_im)
        pi.append(pr[-2] * ab_im + pi[-1] * ab_re)
    pw_re, pw_im = jnp.stack(pr), jnp.stack(pi)
    cr, ci = c_re.astype(F32), c_im.astype(F32)
    rev_re = jnp.stack(pr[chunk - 1::-1])
    rev_im = jnp.stack(pi[chunk - 1::-1])
    w_re = rev_re[..., None] * bb_re[None] - rev_im[..., None] * bb_im[None]
    w_im = rev_re[..., None] * bb_im[None] + rev_im[..., None] * bb_re[None]
    z_re = cr[None] * pw_re[1:, :, None, :] - ci[None] * pw_im[1:, :, None, :]
    z_im = cr[None] * pw_im[1:, :, None, :] + ci[None] * pw_re[1:, :, None, :]
    ab_b_re = pw_re[:chunk, :, :, None] * bb_re[None] - pw_im[:chunk, :, :, None] * bb_im[None]
    ab_b_im = pw_re[:chunk, :, :, None] * bb_im[None] + pw_im[:chunk, :, :, None] * bb_re[None]
    k_tap = (jnp.einsum('gop,tgpi->tgoi', cr, ab_b_re, precision=hp)
             - jnp.einsum('gop,tgpi->tgoi', ci, ab_b_im, precision=hp))

    nblk = g * S5_GROUP // LANES
    gpb = g // nblk
    eye = jnp.eye(gpb, dtype=F32)

    def blockdiag(m, rows_last):
        c_, _, r_, k_ = m.shape
        m5 = m.reshape(c_, nblk, gpb, r_, k_)
        out = m5[:, :, :, :, None, :] * eye[None, None, :, None, :, None]
        return out.transpose(1, 0, 2, 3, 4, 5).reshape(nblk, c_, gpb * r_, gpb * k_)

    w_re_b = _split_stack(blockdiag(jnp.swapaxes(w_re, 2, 3), None))
    w_im_b = _split_stack(blockdiag(jnp.swapaxes(w_im, 2, 3), None))
    v_re_b = _split_stack(blockdiag(jnp.swapaxes(z_re, 2, 3), None))
    v_im_b = _split_stack(blockdiag(jnp.swapaxes(-z_im, 2, 3), None))
    k_b = _split_stack(blockdiag(jnp.swapaxes(k_tap, 2, 3), None))
    al_re = pw_re[chunk].reshape(nblk, 1, gpb * p)
    al_im = pw_im[chunk].reshape(nblk, 1, gpb * p)
    return w_re_b, w_im_b, v_re_b, v_im_b, k_b, al_re, al_im


def _s5_kernel(chunk, scan, u_ref, h0r_ref, h0i_ref, wre_ref, wim_ref, vre_ref, vim_ref, k_ref, ar_ref, ai_ref,
               d_ref, y_ref, hr_out, hi_out, hr_s, hi_s, sr_s, si_s):
    rc = sr_s.shape[0]
    xs = [u_ref[pl.ds(tok, rc, stride=chunk), :] for tok in range(chunk)]
    xb = [_split(x) for x in xs]
    sre = _dot3(xb[0], _pair(wre_ref, 0, 0))
    sim = _dot3(xb[0], _pair(wim_ref, 0, 0))
    for tok in range(1, chunk):
        sre = sre + _dot3(xb[tok], _pair(wre_ref, 0, tok))
        sim = sim + _dot3(xb[tok], _pair(wim_ref, 0, tok))
    ar, ai = ar_ref[0], ai_ref[0]
    if scan:
        @pl.when(pl.program_id(2) == 0)
        def _():
            hr_s[...] = h0r_ref[0, 0]
            hi_s[...] = h0i_ref[0, 0]

        sr_s[...] = sre
        si_s[...] = sim

        def body(c, carry):
            hr, hi = carry
            sr = sr_s[pl.ds(c, 1), :]
            si = si_s[pl.ds(c, 1), :]
            sr_s[pl.ds(c, 1), :] = hr
            si_s[pl.ds(c, 1), :] = hi
            return ar * hr - ai * hi + sr, ar * hi + ai * hr + si

        hr, hi = lax.fori_loop(0, rc, body, (hr_s[...], hi_s[...]))
        hr_s[...] = hr
        hi_s[...] = hi
        hr_out[0, 0] = hr
        hi_out[0, 0] = hi
        h_re, h_im = sr_s[...], si_s[...]
    else:
        h_re, h_im = h0r_ref[0], h0i_ref[0]
        hr_out[0] = ar * h_re - ai * h_im + sre
        hi_out[0] = ar * h_im + ai * h_re + sim
    hb_re, hb_im = _split(h_re), _split(h_im)
    for tok in range(chunk):
        acc = _dot3(hb_re, _pair(vre_ref, 0, tok)) + _dot3(hb_im, _pair(vim_ref, 0, tok))
        for s in range(tok + 1):
            acc = acc + _dot3(xb[s], _pair(k_ref, 0, tok - s))
        acc = acc + d_ref[0] * xs[tok]
        y_ref[pl.ds(tok, rc, stride=chunk), :] = acc


def s5_prompt(proj, h0r, h0i, wts, d_skip, bsz, t, chunk, rc):
    w_re, w_im, v_re, v_im, k_b, al_re, al_im = wts
    n = proj.shape[0]
    nblk = w_re.shape[1]
    sw = w_re.shape[4]
    cw = nblk * LANES
    nt = t // (rc * chunk)
    wspec = lambda a: pl.BlockSpec((2, 1) + a.shape[2:], lambda k, b, j: (0, k) + (0,) * (a.ndim - 2))
    aspec = lambda a: pl.BlockSpec((1,) + a.shape[1:], lambda k, b, j: (k,) + (0,) * (a.ndim - 1))
    hspec = pl.BlockSpec((1, 1, 1, sw), lambda k, b, j: (k, b, 0, 0))
    return pl.pallas_call(
        functools.partial(_s5_kernel, chunk, True),
        grid=(nblk, bsz, nt),
        in_specs=[pl.BlockSpec((rc * chunk, LANES), lambda k, b, j: (b * nt + j, k)),
                  hspec, hspec,
                  wspec(w_re), wspec(w_im), wspec(v_re), wspec(v_im), wspec(k_b), aspec(al_re), aspec(al_im),
                  pl.BlockSpec((1, 1, LANES), lambda k, b, j: (k, 0, 0))],
        out_specs=[pl.BlockSpec((rc * chunk, LANES), lambda k, b, j: (b * nt + j, k)), hspec, hspec],
        out_shape=[jax.ShapeDtypeStruct((n, cw), F32),
                   jax.ShapeDtypeStruct((nblk, bsz, 1, sw), F32),
                   jax.ShapeDtypeStruct((nblk, bsz, 1, sw), F32)],
        scratch_shapes=[pltpu.VMEM((1, sw), F32), pltpu.VMEM((1, sw), F32),
                        pltpu.VMEM((rc, sw), F32), pltpu.VMEM((rc, sw), F32)],
        compiler_params=_params("parallel", "parallel", "arbitrary"),
    )(proj, h0r, h0i, w_re, w_im, v_re, v_im, k_b, al_re, al_im, d_skip.reshape(nblk, 1, LANES))


def s5_sample(proj, h0r, h0i, wts, d_skip, bsz, chunk):
    w_re, w_im, v_re, v_im, k_b, al_re, al_im = wts
    n = proj.shape[0]
    nblk = w_re.shape[1]
    sw = w_re.shape[4]
    cw = nblk * LANES
    wspec = lambda a: pl.BlockSpec((2, 1) + a.shape[2:], lambda k: (0, k) + (0,) * (a.ndim - 2))
    aspec = lambda a: pl.BlockSpec((1,) + a.shape[1:], lambda k: (k,) + (0,) * (a.ndim - 1))
    hspec = pl.BlockSpec((1, bsz, sw), lambda k: (k, 0, 0))
    return pl.pallas_call(
        functools.partial(_s5_kernel, chunk, False),
        grid=(nblk,),
        in_specs=[pl.BlockSpec((n, LANES), lambda k: (0, k)),
                  hspec, hspec,
                  wspec(w_re), wspec(w_im), wspec(v_re), wspec(v_im), wspec(k_b), aspec(al_re), aspec(al_im),
                  pl.BlockSpec((1, 1, LANES), lambda k: (k, 0, 0))],
        out_specs=[pl.BlockSpec((n, LANES), lambda k: (0, k)), hspec, hspec],
        out_shape=[jax.ShapeDtypeStruct((n, cw), F32),
                   jax.ShapeDtypeStruct((nblk, bsz, sw), F32),
                   jax.ShapeDtypeStruct((nblk, bsz, sw), F32)],
        scratch_shapes=[pltpu.VMEM((1, sw), F32), pltpu.VMEM((1, sw), F32),
                        pltpu.VMEM((bsz, sw), F32), pltpu.VMEM((bsz, sw), F32)],
        compiler_params=_params("parallel"),
    )(proj, h0r, h0i, w_re, w_im, v_re, v_im, k_b, al_re, al_im, d_skip.reshape(nblk, 1, LANES))


def _glu_kernel(y_ref, w_ref, b_ref, o_ref):
    z = jax.nn.gelu(y_ref[...], approximate=True)
    o_ref[...] = z * jax.nn.sigmoid(_dot3(z, _pair(w_ref)) + b_ref[...])


def glu(y, w, b, tm):
    n, cw = y.shape
    return pl.pallas_call(
        _glu_kernel,
        grid=(n // tm,),
        in_specs=[pl.BlockSpec((tm, cw), lambda i: (i, 0)),
                  pl.BlockSpec((2, cw, cw), lambda i: (0, 0, 0)),
                  pl.BlockSpec((1, cw), lambda i: (0, 0))],
        out_specs=pl.BlockSpec((tm, cw), lambda i: (i, 0)),
        out_shape=jax.ShapeDtypeStruct((n, cw), F32),
        compiler_params=_params("parallel"),
    )(y, _split_stack(w), b.reshape(1, cw))


def _moba_prompt_kernel(nb, q_ref, k_ref, v_ref, o_ref, kmean_ref):
    i = pl.program_id(2)
    tq = q_ref.shape[0]

    @pl.when(i == 0)
    def _():
        kmean_ref[...] = jnp.zeros_like(kmean_ref)
        for n in range(nb):
            kmean_ref[n:n + 1, :] = jnp.mean(k_ref[n * MOBA_BLOCK:(n + 1) * MOBA_BLOCK, :], axis=0, keepdims=True)

    qb = _split(q_ref[...])
    gate = _dot3_nt(qb, kmean_ref[...])
    blk = lax.broadcasted_iota(I32, gate.shape, 1)
    gate = jnp.where(blk < i, gate, -jnp.inf)
    sel = jnp.zeros(gate.shape, F32)
    for _ in range(MOBA_TOPK):
        top = jnp.max(gate, axis=-1, keepdims=True)
        first = jnp.min(jnp.where(gate == top, blk, nb + LANES), axis=-1, keepdims=True)
        pick = (blk == first) & (first < i)
        sel = jnp.where(pick, 1.0, sel)
        gate = jnp.where(blk == first, -jnp.inf, gate)

    tk = 2 * MOBA_BLOCK
    row = lax.broadcasted_iota(I32, (tq, tk), 0)
    col = lax.broadcasted_iota(I32, (tq, tk), 1)
    second = col >= MOBA_BLOCK
    causal = jnp.where(jnp.where(second, col - MOBA_BLOCK, col) <= row, 1.0, 0.0)

    def att_block(c, carry):
        m, l, acc = carry
        off = pl.multiple_of(c * tk, tk)
        s = _dot3_nt(qb, k_ref[pl.ds(off, tk), :]) * HEAD_DIM ** -0.5
        picked0 = jnp.max(jnp.where(blk == 2 * c, sel, 0.0), axis=-1, keepdims=True)
        picked1 = jnp.max(jnp.where(blk == 2 * c + 1, sel, 0.0), axis=-1, keepdims=True)
        own = jnp.where(second, 2 * c + 1, 2 * c) == i
        valid = jnp.where(own, causal, jnp.where(second, picked1, picked0)) > 0.0
        s = jnp.where(valid, s, NEG_BIG)
        m_new = jnp.maximum(m, jnp.max(s, axis=-1, keepdims=True))
        alpha = jnp.exp(m - m_new)
        p = jnp.where(valid, jnp.exp(s - m_new), 0.0)
        l = alpha * l + jnp.sum(p, axis=-1, keepdims=True)
        acc = alpha * acc + _dot(p.astype(BF16), v_ref[pl.ds(off, tk), :].astype(BF16))
        return m_new, l, acc

    m, l, acc = lax.fori_loop(
        0, i // 2 + 1, att_block,
        (jnp.full((tq, 1), NEG_BIG, F32), jnp.zeros((tq, 1), F32), jnp.zeros((tq, HEAD_DIM), F32)))
    o_ref[...] = acc / l


def moba_prompt(proj, bsz, t):
    n = proj.shape[0]
    n_heads = proj.shape[1] // 4 // HEAD_DIM
    nb = t // MOBA_BLOCK
    assert t % (2 * MOBA_BLOCK) == 0
    nbp = -(-nb // 8) * 8
    return pl.pallas_call(
        functools.partial(_moba_prompt_kernel, nb),
        grid=(bsz, n_heads, nb),
        in_specs=[pl.BlockSpec((MOBA_BLOCK, HEAD_DIM), lambda b, h, i: (b * nb + i, n_heads + h)),
                  pl.BlockSpec((t, HEAD_DIM), lambda b, h, i: (b, 2 * n_heads + h)),
                  pl.BlockSpec((t, HEAD_DIM), lambda b, h, i: (b, 3 * n_heads + h))],
        out_specs=pl.BlockSpec((MOBA_BLOCK, HEAD_DIM), lambda b, h, i: (b * nb + i, h)),
        out_shape=jax.ShapeDtypeStruct((n, n_heads * HEAD_DIM), F32),
        scratch_shapes=[pltpu.VMEM((nbp, HEAD_DIM), F32)],
        compiler_params=_params("parallel", "parallel", "arbitrary"),
    )(proj, proj, proj)


def _router_kernel(x_ref, g_ref, sc_ref, sh_ref, w_ref, b_ref, hn_ref, eid_ref, rank_ref, wt_ref, cnt_ref, carry_ref):
    i = pl.program_id(0)
    tm = x_ref.shape[0]

    @pl.when(i == 0)
    def _():
        carry_ref[...] = jnp.zeros_like(carry_ref)

    hn = _norm_mod(x_ref[...], g_ref[...], sc_ref[0, 0], sh_ref[0, 0])
    hn_ref[...] = hn
    logit = _dot3(hn, w_ref[...]) + b_ref[...]
    lane = lax.broadcasted_iota(I32, logit.shape, 1)
    gmask = lane < N_GROUPS
    gl = jnp.where(gmask, logit, -jnp.inf)
    gmax = jnp.max(gl, axis=-1, keepdims=True)
    gsel = jnp.min(jnp.where(gl == gmax, lane, LANES), axis=-1, keepdims=True)
    g_gate = 1.0 / jnp.sum(jnp.where(gmask, jnp.exp(gl - gmax), 0.0), axis=-1, keepdims=True)
    lo = N_GROUPS + gsel * EXP_PER_GROUP
    emask = (lane >= lo) & (lane < lo + EXP_PER_GROUP)
    el = jnp.where(emask, logit, -jnp.inf)
    v1 = jnp.max(el, axis=-1, keepdims=True)
    i1 = jnp.min(jnp.where(el == v1, lane, LANES), axis=-1, keepdims=True)
    el2 = jnp.where(lane == i1, -jnp.inf, el)
    v2 = jnp.max(el2, axis=-1, keepdims=True)
    i2 = jnp.min(jnp.where(el2 == v2, lane, LANES), axis=-1, keepdims=True)
    e21 = jnp.exp(v2 - v1)
    w1 = g_gate / (1.0 + e21)
    w2 = w1 * e21
    oh1 = jnp.where(lane == i1, 1.0, 0.0)
    oh2 = jnp.where(lane == i2, 1.0, 0.0)
    oh = oh1 + oh2
    r_i = lax.broadcasted_iota(I32, (tm, tm), 0)
    c_i = lax.broadcasted_iota(I32, (tm, tm), 1)
    tri = jnp.where(c_i < r_i, 1.0, 0.0).astype(BF16)
    before = _dot(tri, oh.astype(BF16)) + carry_ref[0:1, :]
    rank1 = jnp.sum(oh1 * before, axis=-1, keepdims=True).astype(I32)
    rank2 = jnp.sum(oh2 * before, axis=-1, keepdims=True).astype(I32)
    eid_ref[...] = jnp.where(lane == 0, i1 - N_GROUPS, jnp.where(lane == 1, i2 - N_GROUPS, 0))
    rank_ref[...] = jnp.where(lane == 0, rank1, jnp.where(lane == 1, rank2, 0))
    wt_ref[...] = jnp.where(lane == 0, w1, jnp.where(lane == 1, w2, 0.0))
    carry_ref[0:1, :] = carry_ref[0:1, :] + jnp.sum(oh, axis=0, keepdims=True)
    cnt_ref[...] = carry_ref[...]


def router(x, g, mod6, w_r, b_r, tm, rows_per_mod):
    n, d = x.shape
    rm = mod6.shape[2]
    tok = lambda width, dt: (pl.BlockSpec((tm, width), lambda i: (i, 0)), jax.ShapeDtypeStruct((n, width), dt))
    specs = [tok(d, F32), tok(LANES, I32), tok(LANES, I32), tok(LANES, F32),
             (pl.BlockSpec((8, LANES), lambda i: (0, 0)), jax.ShapeDtypeStruct((8, LANES), F32))]
    return pl.pallas_call(
        _router_kernel,
        grid=(n // tm,),
        in_specs=[pl.BlockSpec((tm, d), lambda i: (i, 0)),
                  pl.BlockSpec((1, d), lambda i: (0, 0)),
                  _mod_spec(4, tm, rows_per_mod, rm, d, 1),
                  _mod_spec(3, tm, rows_per_mod, rm, d, 1),
                  pl.BlockSpec((d, LANES), lambda i: (0, 0)),
                  pl.BlockSpec((1, LANES), lambda i: (0, 0))],
        out_specs=[s for s, _ in specs],
        out_shape=[o for _, o in specs],
        scratch_shapes=[pltpu.VMEM((8, LANES), F32)],
        compiler_params=_params("arbitrary"),
    )(x, g.reshape(1, d), mod6, mod6, w_r, b_r)


def _expert_kernel(te_ref, tv_ref, src_ref, dst_ref, hn_hbm, w1_ref, w3_ref, w2_ref, y_hbm, xbuf, ybuf, gsem, ssem):
    j = pl.program_id(0)
    nj = pl.num_programs(0)
    tm = ybuf.shape[0]

    def gather_rows(tile, op):
        slot = tile % 2

        def body(r, c):
            op(pltpu.make_async_copy(hn_hbm.at[pl.ds(src_ref[tile * tm + r], 1)], xbuf.at[slot, pl.ds(r, 1)],
                                     gsem.at[slot]))
            return c
        lax.fori_loop(0, tm, body, 0, unroll=8)

    def scatter_rows(tile, op):
        def body(r, c):
            row = dst_ref[tile * tm + r]

            @pl.when(row >= 0)
            def _():
                op(pltpu.make_async_copy(ybuf.at[pl.ds(r, 1)], y_hbm.at[pl.ds(row, 1)], ssem))
            return c
        lax.fori_loop(0, tm, body, 0, unroll=8)

    start = lambda c: c.start()
    wait = lambda c: c.wait()

    @pl.when((j == 0) & (tv_ref[0] > 0))
    def _():
        gather_rows(0, start)

    nxt = jnp.minimum(j + 1, nj - 1)

    @pl.when((j + 1 < nj) & (tv_ref[nxt] > 0))
    def _():
        gather_rows(j + 1, start)

    prev = jnp.maximum(j - 1, 0)

    @pl.when((j > 0) & (tv_ref[prev] > 0))
    def _():
        scatter_rows(j - 1, wait)

    @pl.when(tv_ref[j] > 0)
    def _():
        gather_rows(j, wait)
        xb = _split(xbuf[j % 2])
        a = _dot3(xb, w1_ref[0, 0])
        b = _dot3(xb, w3_ref[0, 0])
        h = (a * jax.nn.sigmoid(a)) * b
        ybuf[...] = _dot3(h, w2_ref[0, 0])
        scatter_rows(j, start)

        @pl.when(j == nj - 1)
        def _():
            scatter_rows(j, wait)


def experts(hn, tile_e, tile_v, src, dst, w1, w3, w2, layer, tm, n_out):
    n_tiles = tile_e.shape[0]
    d = hn.shape[1]
    ff = w1.shape[3]
    grid_spec = pltpu.PrefetchScalarGridSpec(
        num_scalar_prefetch=4,
        grid=(n_tiles,),
        in_specs=[pl.BlockSpec(memory_space=pl.ANY),
                  pl.BlockSpec((1, 1, d, ff), lambda j, te, tv, s, t: (layer, te[j], 0, 0)),
                  pl.BlockSpec((1, 1, d, ff), lambda j, te, tv, s, t: (layer, te[j], 0, 0)),
                  pl.BlockSpec((1, 1, ff, d), lambda j, te, tv, s, t: (layer, te[j], 0, 0))],
        out_specs=pl.BlockSpec(memory_space=pl.ANY),
        scratch_shapes=[pltpu.VMEM((2, tm, d), F32), pltpu.VMEM((tm, d), F32),
                        pltpu.SemaphoreType.DMA((2,)), pltpu.SemaphoreType.DMA(())],
    )
    return pl.pallas_call(
        _expert_kernel,
        grid_spec=grid_spec,
        out_shape=jax.ShapeDtypeStruct((n_out, d), F32),
        compiler_params=_params("arbitrary"),
    )(tile_e, tile_v, src, dst, hn, w1, w3, w2)


def _combine_kernel(x_ref, g_ref, wt_ref, y0_ref, y1_ref, o_ref):
    wt = wt_ref[...]
    o_ref[...] = x_ref[...] + g_ref[0, 0] * (wt[:, 0:1] * y0_ref[...] + wt[:, 1:2] * y1_ref[...])


def combine(x, mod6, wt, y, tm, rows_per_mod):
    n, d = x.shape
    rm = mod6.shape[2]
    nt = n // tm
    return pl.pallas_call(
        _combine_kernel,
        grid=(nt,),
        in_specs=[pl.BlockSpec((tm, d), lambda i: (i, 0)),
                  _mod_spec(5, tm, rows_per_mod, rm, d, 1),
                  pl.BlockSpec((tm, LANES), lambda i: (i, 0)),
                  pl.BlockSpec((tm, d), lambda i: (i, 0)),
                  pl.BlockSpec((tm, d), lambda i: (i + nt, 0))],
        out_specs=pl.BlockSpec((tm, d), lambda i: (i, 0)),
        out_shape=jax.ShapeDtypeStruct((n, d), F32),
        compiler_params=_params("parallel"),
    )(x, mod6, wt, y, y)


def hier_moe(x, g, mod6, w_r, b_r, w1, w3, w2, layer, tm, rows_per_mod, tm_e):
    n, d = x.shape
    hn, eid, rank, wt, cnt = router(x, g, mod6, w_r, b_r, tm, rows_per_mod)
    cnt = cnt[0, N_GROUPS:N_GROUPS + N_EXPERTS].astype(I32)
    padded = ((cnt + tm_e - 1) // tm_e) * tm_e
    ends = jnp.cumsum(padded)
    offs = ends - padded
    e2, r2 = eid[:, :MOE_TOPK], rank[:, :MOE_TOPK]
    pos = (offs[e2] + r2).reshape(-1)
    n_tiles = (n * MOE_TOPK) // tm_e + N_EXPERTS
    n_rows = n_tiles * tm_e
    tok_id = jnp.repeat(jnp.arange(n, dtype=I32), MOE_TOPK)
    pick_id = jnp.tile(jnp.arange(MOE_TOPK, dtype=I32), n)
    dst = jnp.full((n_rows,), -1, I32).at[pos].set(pick_id * n + tok_id)
    src = jnp.where(dst >= 0, dst % n, 0)
    starts = jnp.arange(n_tiles, dtype=I32) * tm_e
    tile_e = jnp.minimum(jnp.sum((starts[:, None] >= ends[None, :]).astype(I32), axis=1), N_EXPERTS - 1)
    tile_v = (starts < ends[-1]).astype(I32)
    n_out = n * MOE_TOPK
    y = experts(hn, tile_e, tile_v, src, dst, w1, w3, w2, layer, tm_e, n_out)
    return combine(x, mod6, wt, y, tm, rows_per_mod)


SAMPLE_ROWS = 8


def _dsa_select_kernel(layer, n_pages, n_sel, t_new, pt_ref, iq_ref, iw_ref, ikn_ref, kidx_hbm,
                       idx_ref, cnt_ref, mnew_ref, kbuf, score_ref, key_ref, pfx_ref, sem):
    b = pl.program_id(0)
    n_past = n_pages * PAGE_SIZE
    tp = SAMPLE_ROWS
    ck = min(2048, n_past)

    def page_copy(pg):
        return pltpu.make_async_copy(kidx_hbm.at[layer, pt_ref[b * n_pages + pg]],
                                     kbuf.at[pl.ds(pg * PAGE_SIZE, PAGE_SIZE)], sem)

    def start(pg, c):
        page_copy(pg).start()
        return c

    def wait(pg, c):
        page_copy(pg).wait()
        return c

    lax.fori_loop(0, n_pages, start, 0)
    lax.fori_loop(0, n_pages, wait, 0)

    iq = _split(iq_ref[0])
    iw = iw_ref[0]

    def head_sum(s):
        s = jnp.maximum(s * IDX_DIM ** -0.5, 0.0) * iw
        rows = [jnp.sum(s[t * IDX_HEADS:(t + 1) * IDX_HEADS], axis=0, keepdims=True) for t in range(tp)]
        return jnp.concatenate(rows, axis=0) * IDX_HEADS ** -0.5

    def score_chunk(c, carry):
        off = pl.multiple_of(c * ck, ck)
        sc = head_sum(_dot3_nt(iq, kbuf[pl.ds(off, ck), :]))
        score_ref[:, pl.ds(off, ck)] = sc
        key_ref[:, pl.ds(off, ck)] = _float_key(sc)
        return carry

    lax.fori_loop(0, n_past // ck, score_chunk, 0)
    ikn = jnp.concatenate([ikn_ref[0], jnp.zeros((LANES - tp, IDX_DIM), F32)], axis=0)
    s_new = head_sum(_dot3_nt(iq, ikn))
    trow = lax.broadcasted_iota(I32, (tp, LANES), 0)
    jcol = lax.broadcasted_iota(I32, (tp, LANES), 1)
    s_new = jnp.where((jcol <= trow) & (jcol < t_new), s_new, -jnp.inf)
    key_new = _float_key(s_new)

    def count_ge(cand):
        def body(c, acc):
            off = pl.multiple_of(c * ck, ck)
            m = jnp.where(key_ref[:, pl.ds(off, ck)] >= cand, 1.0, 0.0)
            for q in range(ck // LANES):
                acc = acc + m[:, q * LANES:(q + 1) * LANES]
            return acc
        acc = lax.fori_loop(0, n_past // ck, body, jnp.where(key_new >= cand, 1.0, 0.0))
        return jnp.sum(acc, axis=-1, keepdims=True)

    def bit_step(t, cur):
        cand = cur + lax.shift_left(jnp.int32(1), 31 - t)
        return jnp.where(count_ge(cand) >= n_sel, cand, cur)

    thr = lax.fori_loop(0, 32, bit_step, jnp.full((tp, 1), INT_MIN, I32))
    mnew_ref[0] = jnp.where((key_new >= thr) & (s_new > -jnp.inf), 1.0, 0.0)

    pb = 256
    r_i = lax.broadcasted_iota(I32, (pb, pb), 0)
    c_i = lax.broadcasted_iota(I32, (pb, pb), 1)
    upper = jnp.where(r_i < c_i, 1.0, 0.0).astype(BF16)

    def prefix_block(c, run):
        off = pl.multiple_of(c * pb, pb)
        m = jnp.where(key_ref[:, pl.ds(off, pb)] >= thr, 1.0, 0.0)
        before = _dot(m.astype(BF16), upper) + run
        pfx_ref[:, pl.ds(off, pb)] = jnp.where(m > 0.0, before, -1.0)
        return run + jnp.sum(m, axis=-1, keepdims=True)

    total = lax.fori_loop(0, n_past // pb, prefix_block, jnp.zeros((tp, 1), F32))
    cnt_ref[0] = jnp.broadcast_to(jnp.minimum(total, float(n_sel)), (tp, LANES))

    slot = lax.broadcasted_iota(I32, (n_sel, ck), 0).astype(F32)
    kpos = lax.broadcasted_iota(I32, (n_sel, ck), 1).astype(F32)
    for t in range(t_new):
        def slot_chunk(c, acc, t=t):
            off = pl.multiple_of(c * ck, ck)
            pf = pfx_ref[t:t + 1, pl.ds(off, ck)]
            hit = jnp.where(pf == slot, kpos + jnp.asarray(c * ck, F32), 0.0)
            return acc + jnp.sum(hit, axis=-1, keepdims=True)
        pos = lax.fori_loop(0, n_past // ck, slot_chunk, jnp.zeros((n_sel, 1), F32))
        idx_ref[0, t] = pos.astype(I32)


def dsa_select(iq, iw, ik_new, pool_kidx, page_table, layer, t_new):
    bsz = iq.shape[0]
    n_pages = page_table.shape[1]
    n_past = n_pages * PAGE_SIZE
    n_sel = min(IDX_TOPK, (n_past + t_new) // 4)
    tp = SAMPLE_ROWS
    grid_spec = pltpu.PrefetchScalarGridSpec(
        num_scalar_prefetch=1,
        grid=(bsz,),
        in_specs=[pl.BlockSpec((1, tp * IDX_HEADS, IDX_DIM), lambda b, pt: (b, 0, 0)),
                  pl.BlockSpec((1, tp * IDX_HEADS, 1), lambda b, pt: (b, 0, 0)),
                  pl.BlockSpec((1, tp, IDX_DIM), lambda b, pt: (b, 0, 0)),
                  pl.BlockSpec(memory_space=pl.ANY)],
        out_specs=[pl.BlockSpec((1, t_new, n_sel, 1), lambda b, pt: (b, 0, 0, 0)),
                   pl.BlockSpec((1, tp, LANES), lambda b, pt: (b, 0, 0)),
                   pl.BlockSpec((1, tp, LANES), lambda b, pt: (b, 0, 0))],
        scratch_shapes=[pltpu.VMEM((n_past, IDX_DIM), F32), pltpu.VMEM((tp, n_past), F32),
                        pltpu.VMEM((tp, n_past), I32), pltpu.VMEM((tp, n_past), F32),
                        pltpu.SemaphoreType.DMA(())],
    )
    return pl.pallas_call(
        functools.partial(_dsa_select_kernel, layer, n_pages, n_sel, t_new),
        grid_spec=grid_spec,
        out_shape=[jax.ShapeDtypeStruct((bsz, t_new, n_sel, 1), I32),
                   jax.ShapeDtypeStruct((bsz, tp, LANES), F32),
                   jax.ShapeDtypeStruct((bsz, tp, LANES), F32)],
        compiler_params=_params("arbitrary"),
    )(page_table.reshape(-1), iq, iw, ik_new, pool_kidx)


def _dsa_gather_kernel(layer, n_pages, n_sel, t_new, pt_ref, idx_ref, cnt_ref, mnew_ref, q_ref, kn_ref, vn_ref,
                       k_hbm, v_hbm, o_ref, kbuf, vbuf, ksem, vsem):
    b = pl.program_id(0)
    t = pl.program_id(1)
    base = (b * t_new + t) * n_sel

    def copies(j):
        pos = idx_ref[base + j]
        page = pt_ref[b * n_pages + pos // PAGE_SIZE]
        row = pos % PAGE_SIZE
        return (pltpu.make_async_copy(k_hbm.at[layer, page, pl.ds(row, 1)], kbuf.at[pl.ds(j, 1)], ksem),
                pltpu.make_async_copy(v_hbm.at[layer, page, pl.ds(row, 1)], vbuf.at[pl.ds(j, 1)], vsem))

    def start(j, c):
        ck, cv = copies(j)
        ck.start()
        cv.start()
        return c

    def wait(j, c):
        ck, cv = copies(j)
        ck.wait()
        cv.wait()
        return c

    lax.fori_loop(0, n_sel, start, 0)
    lax.fori_loop(0, n_sel, wait, 0)

    q = q_ref[0]
    scale = HEAD_DIM ** -0.5
    s = jnp.sum(kbuf[...] * q, axis=-1, keepdims=True) * scale
    slot = lax.broadcasted_iota(I32, s.shape, 0)
    s = jnp.where(slot < cnt_ref[b * t_new + t], s, NEG_BIG)
    kn = kn_ref[0]
    s_n = jnp.sum(kn * q, axis=-1, keepdims=True) * scale
    tn = lax.broadcasted_iota(I32, s_n.shape, 0)
    ok = jnp.zeros(s_n.shape, F32)
    for j in range(t_new):
        ok = jnp.where(tn == j, mnew_ref[(b * t_new + t) * t_new + j].astype(F32), ok)
    s_n = jnp.where(ok > 0.0, s_n, NEG_BIG)
    m = jnp.maximum(jnp.max(s, axis=0, keepdims=True), jnp.max(s_n, axis=0, keepdims=True))
    p = jnp.exp(s - m)
    p_n = jnp.exp(s_n - m)
    l = jnp.sum(p, axis=0, keepdims=True) + jnp.sum(p_n, axis=0, keepdims=True)
    acc = jnp.sum(p * vbuf[...], axis=0, keepdims=True) + jnp.sum(p_n * vn_ref[0], axis=0, keepdims=True)
    o_ref[0] = acc / l


def dsa_gather(q, k_new, v_new, idx, cnt, mnew, pool_k, pool_v, page_table, layer):
    bsz, t_new, n_heads, dh = q.shape
    n_pages = page_table.shape[1]
    n_sel = idx.shape[0] // (bsz * t_new)
    grid_spec = pltpu.PrefetchScalarGridSpec(
        num_scalar_prefetch=4,
        grid=(bsz, t_new),
        in_specs=[pl.BlockSpec((1, 1, n_heads, dh), lambda b, t, *_: (b, t, 0, 0)),
                  pl.BlockSpec((1, t_new, n_heads, dh), lambda b, t, *_: (b, 0, 0, 0)),
                  pl.BlockSpec((1, t_new, n_heads, dh), lambda b, t, *_: (b, 0, 0, 0)),
                  pl.BlockSpec(memory_space=pl.ANY), pl.BlockSpec(memory_space=pl.ANY)],
        out_specs=pl.BlockSpec((1, 1, n_heads, dh), lambda b, t, *_: (b, t, 0, 0)),
        scratch_shapes=[pltpu.VMEM((n_sel, n_heads, dh), F32), pltpu.VMEM((n_sel, n_heads, dh), F32),
                        pltpu.SemaphoreType.DMA(()), pltpu.SemaphoreType.DMA(())],
    )
    return pl.pallas_call(
        functools.partial(_dsa_gather_kernel, layer, n_pages, n_sel, t_new),
        grid_spec=grid_spec,
        out_shape=jax.ShapeDtypeStruct((bsz, t_new, n_heads, dh), F32),
        compiler_params=_params("arbitrary", "arbitrary"),
    )(page_table.reshape(-1), idx, cnt, mnew, q, k_new, v_new, pool_k, pool_v)


def dsa_sample(proj, ikiw, pool_k, pool_v, pool_kidx, page_table, layer, bsz, t_new):
    aw = pool_k.shape[3] * pool_k.shape[4]
    n_heads = pool_k.shape[3]
    tp = SAMPLE_ROWS
    hs = (bsz, t_new, n_heads, HEAD_DIM)
    pad_t = lambda a: jnp.concatenate([a, jnp.zeros((bsz, tp - t_new) + a.shape[2:], a.dtype)], axis=1)
    iq = pad_t(proj[:, 3 * aw:3 * aw + IDX_HEADS * IDX_DIM].reshape(bsz, t_new, IDX_HEADS, IDX_DIM))
    iw = pad_t(ikiw[:, IDX_DIM:IDX_DIM + IDX_HEADS].reshape(bsz, t_new, IDX_HEADS))
    ik_new = pad_t(ikiw[:, :IDX_DIM].reshape(bsz, t_new, IDX_DIM))
    idx, cnt, mnew = dsa_select(iq.reshape(bsz, tp * IDX_HEADS, IDX_DIM), iw.reshape(bsz, tp * IDX_HEADS, 1),
                                ik_new, pool_kidx, page_table, layer, t_new)
    cnt = cnt[:, :t_new, 0].astype(I32).reshape(-1)
    mnew = mnew[:, :t_new, :t_new].astype(I32).reshape(-1)
    out = dsa_gather(proj[:, :aw].reshape(hs), proj[:, aw:2 * aw].reshape(hs), proj[:, 2 * aw:3 * aw].reshape(hs),
                     idx.reshape(-1), cnt, mnew, pool_k, pool_v, page_table, layer)
    return out.reshape(bsz * t_new, aw)


def _moba_mean_kernel(ppb, bps, pt_ref, *refs):
    pages, o_ref = refs[:ppb * bps], refs[ppb * bps]
    for s in range(bps):
        acc = jnp.sum(pages[s * ppb][0, 0], axis=0)
        for pg in pages[s * ppb + 1:(s + 1) * ppb]:
            acc = acc + jnp.sum(pg[0, 0], axis=0)
        o_ref[0, s] = acc * (1.0 / MOBA_BLOCK)


def moba_block_means(pool_k, page_table, layer):
    bsz, n_pages = page_table.shape
    _, _, page, n_heads, dh = pool_k.shape
    ppb = MOBA_BLOCK // page
    nbp = n_pages // ppb
    bps = 4 if nbp % 4 == 0 else 1

    def page_spec(j):
        return pl.BlockSpec((1, 1, page, n_heads, dh),
                            lambda b, n, pt: (layer, pt[b * n_pages + n * ppb * bps + j], 0, 0, 0))

    grid_spec = pltpu.PrefetchScalarGridSpec(
        num_scalar_prefetch=1,
        grid=(bsz, nbp // bps),
        in_specs=[page_spec(j) for j in range(ppb * bps)],
        out_specs=pl.BlockSpec((1, bps, n_heads, dh), lambda b, n, pt: (b, n, 0, 0)),
    )
    return pl.pallas_call(
        functools.partial(_moba_mean_kernel, ppb, bps),
        grid_spec=grid_spec,
        out_shape=jax.ShapeDtypeStruct((bsz, nbp, n_heads, dh), F32),
        compiler_params=_params("parallel", "arbitrary"),
    )(page_table.reshape(-1), *([pool_k] * (ppb * bps)))


def _moba_pick_kernel(n_top, km_ref, q_ref, sel_ref):
    km = km_ref[0]
    nbp = km.shape[0]
    t_new = q_ref.shape[1]
    for t in range(t_new):
        s = jnp.sum(km * q_ref[0, t:t + 1], axis=-1, keepdims=True)
        blk = lax.broadcasted_iota(I32, s.shape, 0)
        for j in range(n_top):
            top = jnp.max(s, axis=0, keepdims=True)
            first = jnp.min(jnp.where(s == top, blk, nbp), axis=0, keepdims=True)
            sel_ref[0, t, j] = first[0]
            s = jnp.where(blk == first, -jnp.inf, s)


def moba_pick(kmean, q, n_top):
    bsz, nbp, n_heads, dh = kmean.shape
    t_new = q.shape[1]
    return pl.pallas_call(
        functools.partial(_moba_pick_kernel, n_top),
        grid=(bsz,),
        in_specs=[pl.BlockSpec((1, nbp, n_heads, dh), lambda b: (b, 0, 0, 0)),
                  pl.BlockSpec((1, t_new, n_heads, dh), lambda b: (b, 0, 0, 0))],
        out_specs=pl.BlockSpec((1, t_new, n_top, n_heads, 1), lambda b: (b, 0, 0, 0, 0)),
        out_shape=jax.ShapeDtypeStruct((bsz, t_new, n_top, n_heads, 1), I32),
        compiler_params=_params("parallel"),
    )(kmean, q)


def _moba_gather_kernel(layer, n_pages, n_top, ppb, pt_ref, sel_ref, q_ref, kn_ref, vn_ref, k_hbm, v_hbm, o_ref,
                        kbuf, vbuf, ksem, vsem):
    b = pl.program_id(0)
    t = pl.program_id(1)
    t_new = pl.num_programs(1)
    n_heads = kbuf.shape[0]
    page = PAGE_SIZE
    copies = []
    for h in range(n_heads):
        for j in range(n_top):
            blk = sel_ref[((b * t_new + t) * n_top + j) * n_heads + h]
            for pg in range(ppb):
                phys = pt_ref[b * n_pages + blk * ppb + pg]
                dst = pl.ds((j * ppb + pg) * page, page)
                copies.append(pltpu.make_async_copy(k_hbm.at[layer, phys, :, h, :], kbuf.at[h, dst], ksem))
                copies.append(pltpu.make_async_copy(v_hbm.at[layer, phys, :, h, :], vbuf.at[h, dst], vsem))
    for c in copies:
        c.start()
    for c in copies:
        c.wait()

    q = q_ref[0, 0]
    scale = HEAD_DIM ** -0.5
    s = jnp.sum(kbuf[...] * q, axis=-1, keepdims=True) * scale
    s_n = jnp.sum(kn_ref[0] * q, axis=-1, keepdims=True) * scale
    tn = lax.broadcasted_iota(I32, s_n.shape, 1)
    s_n = jnp.where(tn <= t, s_n, NEG_BIG)
    m = jnp.maximum(jnp.max(s, axis=1, keepdims=True), jnp.max(s_n, axis=1, keepdims=True))
    p = jnp.exp(s - m)
    p_n = jnp.exp(s_n - m)
    l = jnp.sum(p, axis=1, keepdims=True) + jnp.sum(p_n, axis=1, keepdims=True)
    acc = jnp.sum(p * vbuf[...], axis=1, keepdims=True) + jnp.sum(p_n * vn_ref[0], axis=1, keepdims=True)
    o_ref[0, 0] = acc / l


def moba_gather(q, k_new, v_new, sel, pool_k, pool_v, page_table, layer, n_top):
    bsz, t_new, n_heads, _, dh = q.shape
    n_pages = page_table.shape[1]
    ppb = MOBA_BLOCK // PAGE_SIZE
    grid_spec = pltpu.PrefetchScalarGridSpec(
        num_scalar_prefetch=2,
        grid=(bsz, t_new),
        in_specs=[pl.BlockSpec((1, 1, n_heads, 1, dh), lambda b, t, *_: (b, t, 0, 0, 0)),
                  pl.BlockSpec((1, n_heads, t_new, dh), lambda b, t, *_: (b, 0, 0, 0)),
                  pl.BlockSpec((1, n_heads, t_new, dh), lambda b, t, *_: (b, 0, 0, 0)),
                  pl.BlockSpec(memory_space=pl.ANY), pl.BlockSpec(memory_space=pl.ANY)],
        out_specs=pl.BlockSpec((1, 1, n_heads, 1, dh), lambda b, t, *_: (b, t, 0, 0, 0)),
        scratch_shapes=[pltpu.VMEM((n_heads, n_top * MOBA_BLOCK, dh), F32),
                        pltpu.VMEM((n_heads, n_top * MOBA_BLOCK, dh), F32),
                        pltpu.SemaphoreType.DMA(()), pltpu.SemaphoreType.DMA(())],
    )
    return pl.pallas_call(
        functools.partial(_moba_gather_kernel, layer, n_pages, n_top, ppb),
        grid_spec=grid_spec,
        out_shape=jax.ShapeDtypeStruct((bsz, t_new, n_heads, 1, dh), F32),
        compiler_params=_params("arbitrary", "arbitrary"),
    )(page_table.reshape(-1), sel, q, k_new, v_new, pool_k, pool_v)


def moba_sample(proj, pool_k, pool_v, page_table, layer, bsz, t_new):
    n_heads = pool_k.shape[3]
    aw = n_heads * HEAD_DIM
    n_past = page_table.shape[1] * PAGE_SIZE
    assert n_past % MOBA_BLOCK == 0 and t_new <= MOBA_BLOCK
    n_top = min(MOBA_TOPK, n_past // MOBA_BLOCK)
    hs = (bsz, t_new, n_heads, HEAD_DIM)
    q = proj[:, aw:2 * aw].reshape(hs)
    k_new = proj[:, 2 * aw:3 * aw].reshape(hs).transpose(0, 2, 1, 3)
    v_new = proj[:, 3 * aw:4 * aw].reshape(hs).transpose(0, 2, 1, 3)
    kmean = moba_block_means(pool_k, page_table, layer)
    sel = moba_pick(kmean, q, n_top)
    out = moba_gather(q[:, :, :, None, :], k_new, v_new, sel.reshape(-1), pool_k, pool_v, page_table, layer, n_top)
    return out.reshape(bsz * t_new, aw)


def _ab_weight(w):
    d = w.shape[0]
    aw = (w.shape[1] - IDX_HEADS * IDX_DIM - IDX_DIM - IDX_HEADS) // 6
    main = 3 * aw + IDX_HEADS * IDX_DIM
    small = IDX_DIM + IDX_HEADS
    pad = 512 - small
    return _split_stack(jnp.concatenate([w[:, :main], w[:, main + small:], w[:, main:main + small],
                                         jnp.zeros((d, pad), w.dtype)], axis=1))


def _router_weight(wg, bg, we, be):
    d = wg.shape[0]
    pad = LANES - N_GROUPS - N_EXPERTS
    w = jnp.concatenate([wg, we, jnp.zeros((d, pad), F32)], axis=1)
    b = jnp.concatenate([bg, be, jnp.zeros((pad,), F32)]).reshape(1, LANES)
    return w, b


class _Tiles(NamedTuple):
    rows: int
    rows_per_mod: int
    expert_rows: int
    dsa_q: int = 256
    dsa_k: int = 512
    conv_rows: int = 512
    s5_chunk: int = 8
    s5_chunks: int = 64


def _tiles(prompt, n, t):
    if prompt:
        return _Tiles(rows=512, rows_per_mod=t, expert_rows=256)
    return _Tiles(rows=n, rows_per_mod=n, expert_rows=16, s5_chunk=t)


def _trunk(x, mod, p, past, prompt):
    bsz, t, d = x.shape
    n = bsz * t
    depth = p['norm1'].shape[0]
    aw = d // 2
    n_heads = aw // HEAD_DIM
    tiles = _tiles(prompt, n, t)
    tm, rows_per_mod, tm_e = tiles.rows, tiles.rows_per_mod, tiles.expert_rows
    xf = x.reshape(n, d)
    ab_states, cd_states = [], []
    for l in range(depth):
        i = l // 2
        m6 = mod[l].reshape(bsz, 6, d).transpose(1, 0, 2)
        if prompt:
            mod6 = m6[:, :, None, :]
        else:
            mod6 = jnp.repeat(m6, t, axis=1)[:, None, :, :]
        if l % 2 == 0:
            proj = norm_proj(xf, p['norm1'][l], mod6, 1, 0, _ab_weight(p['w_in_ab'][i]), tm, rows_per_mod)
            k = proj[:, aw:2 * aw]
            v = proj[:, 2 * aw:3 * aw]
            ikiw = proj[:, 7 * aw:7 * aw + LANES]
            ik = ikiw[:, :IDX_DIM]
            if prompt:
                zeros = jnp.zeros_like(ik)
                ik_e = _split_stack(jnp.concatenate([ik, zeros], axis=1))
                ik_o = _split_stack(jnp.concatenate([zeros, ik], axis=1))
                a = dsa_prompt(proj, ik_e, ik_o, n_heads, bsz, t, tiles.dsa_q, tiles.dsa_k)
                b, nbuf = conv_prompt(proj, p['conv_w'][i], bsz, t, tiles.conv_rows)
                new_buf = nbuf[:, 8 - (CONV_W - 1):]
            else:
                a = dsa_sample(proj, ikiw, past['cache_a_k'], past['cache_a_v'], past['cache_a_kidx'],
                               past['page_table'], i, bsz, t)
                b, u = conv_sample(proj, p['conv_w'][i], past['state_b_conv'][i], t)
                new_buf = u.reshape(bsz, t, aw)[:, t - (CONV_W - 1):]
            xf = out_proj(a, b, p['w_out_ab'][i], xf, mod6, 2, tm, rows_per_mod)
            ab_states.append((k.reshape(bsz, t, n_heads, HEAD_DIM), v.reshape(bsz, t, n_heads, HEAD_DIM),
                              ik.reshape(bsz, t, IDX_DIM), new_buf))
        else:
            proj = norm_proj(xf, p['norm1'][l], mod6, 1, 0, _split_stack(p['w_in_cd'][i]), tm, rows_per_mod)
            k = proj[:, 2 * aw:3 * aw]
            v = proj[:, 3 * aw:4 * aw]
            chunk = tiles.s5_chunk
            wts = s5_weights(p['s5_log_dt'][i], p['s5_a_re'][i], p['s5_a_im'][i], p['s5_b_re'][i], p['s5_b_im'][i],
                             p['s5_c_re'][i], p['s5_c_im'][i], chunk)
            nblk = aw // LANES
            if prompt:
                h0 = jnp.zeros((nblk, bsz, 1, (aw // S5_GROUP) * S5_STATE // nblk), F32)
                y, hr, hi = s5_prompt(proj, h0, h0, wts, p['s5_d'][i], bsz, t, chunk, tiles.s5_chunks)
                hr, hi = hr[:, :, 0], hi[:, :, 0]
                dd = moba_prompt(proj, bsz, t)
            else:
                to_blk = lambda s: s.reshape(bsz, nblk, -1).transpose(1, 0, 2)
                y, hr, hi = s5_sample(proj, to_blk(past['state_c_re'][i]), to_blk(past['state_c_im'][i]), wts,
                                      p['s5_d'][i], bsz, chunk)
                dd = moba_sample(proj, past['cache_d_k'], past['cache_d_v'], past['page_table'], i, bsz, t)
            from_blk = lambda s: s.transpose(1, 0, 2).reshape(bsz, aw // S5_GROUP, S5_STATE)
            c = glu(y, p['glu_w'][i], p['glu_b'][i], tm)
            xf = out_proj(c, dd, p['w_out_cd'][i], xf, mod6, 2, tm, rows_per_mod)
            cd_states.append((from_blk(hr), from_blk(hi), k.reshape(bsz, t, n_heads, HEAD_DIM),
                              v.reshape(bsz, t, n_heads, HEAD_DIM)))
        w_r, b_r = _router_weight(p['router_g_w'][l], p['router_g_b'][l], p['router_e_w'][l], p['router_e_b'][l])
        xf = hier_moe(xf, p['norm2'][l], mod6, w_r, b_r, p['exp_w1'], p['exp_w3'], p['exp_w2'], l, tm,
                      rows_per_mod, tm_e)
    y = final_norm(xf, p['final_norm'], tm).reshape(bsz, t, d)
    a_k, a_v, a_kidx, b_conv = [jnp.stack(s) for s in zip(*ab_states)]
    c_re, c_im, d_k, d_v = [jnp.stack(s) for s in zip(*cd_states)]
    return y, a_k, a_v, a_kidx, b_conv, c_re, c_im, d_k, d_v


def kernel(x_prompt, x_sample, cache_a_k, cache_a_v, cache_a_kidx, state_b_conv, state_c_re, state_c_im, cache_d_k, cache_d_v, page_table, c_prompt, c_sample, norm1, norm2, ada_w, ada_b, w_in_ab, conv_w, w_out_ab, w_in_cd, s5_a_re, s5_a_im, s5_b_re, s5_b_im, s5_c_re, s5_c_im, s5_d, s5_log_dt, glu_w, glu_b, w_out_cd, router_g_w, router_g_b, router_e_w, router_e_b, exp_w1, exp_w3, exp_w2, final_norm):
    p = dict(norm1=norm1, norm2=norm2, w_in_ab=w_in_ab, conv_w=conv_w, w_out_ab=w_out_ab, w_in_cd=w_in_cd,
             s5_a_re=s5_a_re, s5_a_im=s5_a_im, s5_b_re=s5_b_re, s5_b_im=s5_b_im, s5_c_re=s5_c_re, s5_c_im=s5_c_im,
             s5_d=s5_d, s5_log_dt=s5_log_dt, glu_w=glu_w, glu_b=glu_b, w_out_cd=w_out_cd, router_g_w=router_g_w,
             router_g_b=router_g_b, router_e_w=router_e_w, router_e_b=router_e_b, exp_w1=exp_w1, exp_w3=exp_w3,
             exp_w2=exp_w2, final_norm=final_norm)
    past = dict(cache_a_k=cache_a_k, cache_a_v=cache_a_v, cache_a_kidx=cache_a_kidx, state_b_conv=state_b_conv,
                state_c_re=state_c_re, state_c_im=state_c_im, cache_d_k=cache_d_k, cache_d_v=cache_d_v,
                page_table=page_table)
    bp, bs = c_prompt.shape[0], c_sample.shape[0]
    rows = -(-(bp + bs) // 8) * 8
    c_all = jnp.concatenate([c_prompt, c_sample, jnp.zeros((rows - bp - bs, c_prompt.shape[1]), F32)], axis=0)
    mod = ada_mod(c_all, ada_w, ada_b)
    outs_p = _trunk(x_prompt, mod[:, :bp], p, None, True)
    outs_s = _trunk(x_sample, mod[:, bp:bp + bs], p, past, False)
    return (outs_p[0], outs_s[0]) + tuple(outs_p[1:]) + tuple(outs_s[1:])
```

```python
import functools
from typing import NamedTuple

import jax
import jax.numpy as jnp
from jax import lax
from jax.experimental import pallas as pl
from jax.experimental.pallas import tpu as pltpu

F32 = jnp.float32
BF16 = jnp.bfloat16
I32 = jnp.int32

HEAD_DIM = 128
IDX_HEADS = 16
IDX_DIM = 64
IDX_TOPK = 256
CONV_W = 3
S5_GROUP = 16
S5_STATE = 64
MOBA_BLOCK = 256
MOBA_TOPK = 3
N_GROUPS = 4
EXP_PER_GROUP = 8
N_EXPERTS = N_GROUPS * EXP_PER_GROUP
MOE_TOPK = 2
PAGE_SIZE = 128
EPS = 1e-6

LANES = 128
NEG_BIG = -1e30
INT_MIN = -(2 ** 31)
VMEM_LIMIT = 56 * 1024 * 1024


def _params(*sem):
    return pltpu.CompilerParams(dimension_semantics=sem, vmem_limit_bytes=VMEM_LIMIT)


def _dot(a, b):
    return jnp.dot(a, b, preferred_element_type=F32)


def _dot_nt(a, b):
    return lax.dot_general(a, b, (((1,), (1,)), ((), ())), preferred_element_type=F32)


def _split(x):
    hi = x.astype(BF16)
    return hi, (x - hi.astype(F32)).astype(BF16)


def _split_stack(x):
    x = x.astype(F32)
    hi = lax.reduce_precision(x, exponent_bits=8, mantissa_bits=7)
    return jnp.stack([hi.astype(BF16), (x - hi).astype(BF16)])


def _dot3(a, b, dot=_dot):
    ah, al = a if isinstance(a, tuple) else _split(a)
    bh, bl = b if isinstance(b, tuple) else _split(b)
    return dot(ah, bh) + dot(ah, bl) + dot(al, bh)


def _dot3_nt(a, b):
    return _dot3(a, b, _dot_nt)


def _dot2_nt(a, b):
    bh = b.astype(BF16)
    return _dot_nt(a[0], bh) + _dot_nt(a[1], bh)


def _pair(ref, *idx):
    return ref[(0,) + idx], ref[(1,) + idx]


def _ada_kernel(c_ref, w_ref, b_ref, o_ref):
    c = c_ref[...]
    o_ref[0] = _dot3(c * jax.nn.sigmoid(c), w_ref[0]) + b_ref[0]


def ada_mod(c, ada_w, ada_b):
    depth, d, n6 = ada_w.shape
    r = c.shape[0]
    tn = 1024
    return pl.pallas_call(
        _ada_kernel,
        grid=(depth, n6 // tn),
        in_specs=[pl.BlockSpec((r, d), lambda l, j: (0, 0)),
                  pl.BlockSpec((1, d, tn), lambda l, j: (l, 0, j)),
                  pl.BlockSpec((1, 1, tn), lambda l, j: (l, 0, j))],
        out_specs=pl.BlockSpec((1, r, tn), lambda l, j: (l, 0, j)),
        out_shape=jax.ShapeDtypeStruct((depth, r, n6), F32),
        compiler_params=_params("parallel", "parallel"),
    )(c, ada_w, ada_b.reshape(depth, 1, n6))


def _norm_mod(x, g, sc, sh):
    y = x * lax.rsqrt(jnp.mean(x * x, axis=-1, keepdims=True) + EPS)
    return (y * g) * (1.0 + sc) + sh


def _norm_proj_kernel(x_ref, g_ref, sc_ref, sh_ref, w_ref, o_ref, hn_ref):
    @pl.when(pl.program_id(1) == 0)
    def _():
        hi, lo = _split(_norm_mod(x_ref[...], g_ref[...], sc_ref[0, 0], sh_ref[0, 0]))
        hn_ref[0] = hi
        hn_ref[1] = lo

    o_ref[...] = _dot3(_pair(hn_ref), _pair(w_ref))


def _mod_spec(which, tm, rows_per_mod, rm, d, nargs):
    if nargs == 2:
        return pl.BlockSpec((1, 1, rm, d), lambda i, j: (which, (i * tm) // rows_per_mod, 0, 0))
    return pl.BlockSpec((1, 1, rm, d), lambda i: (which, (i * tm) // rows_per_mod, 0, 0))


def norm_proj(x, g, mod6, sc_idx, sh_idx, w, tm, rows_per_mod):
    n, d = x.shape
    nc = w.shape[2]
    tn = 512
    rm = mod6.shape[2]
    return pl.pallas_call(
        _norm_proj_kernel,
        grid=(n // tm, nc // tn),
        in_specs=[pl.BlockSpec((tm, d), lambda i, j: (i, 0)),
                  pl.BlockSpec((1, d), lambda i, j: (0, 0)),
                  _mod_spec(sc_idx, tm, rows_per_mod, rm, d, 2),
                  _mod_spec(sh_idx, tm, rows_per_mod, rm, d, 2),
                  pl.BlockSpec((2, d, tn), lambda i, j: (0, 0, j))],
        out_specs=pl.BlockSpec((tm, tn), lambda i, j: (i, j)),
        out_shape=jax.ShapeDtypeStruct((n, nc), F32),
        scratch_shapes=[pltpu.VMEM((2, tm, d), BF16)],
        compiler_params=_params("parallel", "arbitrary"),
    )(x, g.reshape(1, d), mod6, mod6, w)


def _out_proj_kernel(a_ref, b_ref, wa_ref, wb_ref, x_ref, g_ref, o_ref):
    y = _dot3(a_ref[...], _pair(wa_ref)) + _dot3(b_ref[...], _pair(wb_ref))
    o_ref[...] = x_ref[...] + g_ref[0, 0] * y


def out_proj(a, b, w, x, mod6, g_idx, tm, rows_per_mod):
    n, d = x.shape
    ka, kb = a.shape[1], b.shape[1]
    rm = mod6.shape[2]
    tn = d // 2
    wa, wb = _split_stack(w[:ka]), _split_stack(w[ka:])
    return pl.pallas_call(
        _out_proj_kernel,
        grid=(n // tm, d // tn),
        in_specs=[pl.BlockSpec((tm, ka), lambda i, j: (i, 0)),
                  pl.BlockSpec((tm, kb), lambda i, j: (i, 0)),
                  pl.BlockSpec((2, ka, tn), lambda i, j: (0, 0, j)),
                  pl.BlockSpec((2, kb, tn), lambda i, j: (0, 0, j)),
                  pl.BlockSpec((tm, tn), lambda i, j: (i, j)),
                  pl.BlockSpec((1, 1, rm, tn), lambda i, j: (g_idx, (i * tm) // rows_per_mod, 0, j))],
        out_specs=pl.BlockSpec((tm, tn), lambda i, j: (i, j)),
        out_shape=jax.ShapeDtypeStruct((n, d), F32),
        compiler_params=_params("parallel", "parallel"),
    )(a, b, wa, wb, x, mod6)


def _final_norm_kernel(x_ref, g_ref, o_ref):
    x = x_ref[...]
    o_ref[...] = (x * lax.rsqrt(jnp.mean(x * x, axis=-1, keepdims=True) + EPS)) * g_ref[...]


def final_norm(x, g, tm):
    n, d = x.shape
    return pl.pallas_call(
        _final_norm_kernel,
        grid=(n // tm,),
        in_specs=[pl.BlockSpec((tm, d), lambda i: (i, 0)), pl.BlockSpec((1, d), lambda i: (0, 0))],
        out_specs=pl.BlockSpec((tm, d), lambda i: (i, 0)),
        out_shape=jax.ShapeDtypeStruct((n, d), F32),
        compiler_params=_params("parallel"),
    )(x, g.reshape(1, d))


def _conv_prompt_kernel(gb_ref, gc_ref, gh_ref, pc_ref, ph_ref, w_ref, o_ref, nb_ref):
    j = pl.program_id(1)
    tq = gc_ref.shape[0]
    u = gc_ref[...] * gh_ref[...]
    prev = pc_ref[...] * ph_ref[...]
    prev = jnp.where(j == 0, 0.0, prev)
    row = lax.broadcasted_iota(I32, u.shape, 0)
    u1 = jnp.where(row == 0, prev[7:8], pltpu.roll(u, 1, 0))
    u2 = jnp.where(row == 0, prev[6:7], jnp.where(row == 1, prev[7:8], pltpu.roll(u, 2, 0)))
    y = w_ref[0:1] * u2 + w_ref[1:2] * u1 + w_ref[2:3] * u
    o_ref[...] = (gb_ref[...] * y).astype(o_ref.dtype)
    nb_ref[0] = u[tq - 8:tq]


def conv_prompt(proj, conv_w, bsz, t, tq):
    n = proj.shape[0]
    w = conv_w.shape[1]
    nt = t // tq
    wpad = jnp.concatenate([conv_w, jnp.zeros((8 - CONV_W, w), F32)], axis=0)

    def halo(cb):
        return pl.BlockSpec((8, w), lambda b, j: (jnp.maximum((b * t + j * tq) // 8 - 1, 0), cb))

    return pl.pallas_call(
        _conv_prompt_kernel,
        grid=(bsz, nt),
        in_specs=[pl.BlockSpec((tq, w), lambda b, j: (b * nt + j, 4)),
                  pl.BlockSpec((tq, w), lambda b, j: (b * nt + j, 5)),
                  pl.BlockSpec((tq, w), lambda b, j: (b * nt + j, 6)),
                  halo(5), halo(6),
                  pl.BlockSpec((8, w), lambda b, j: (0, 0))],
        out_specs=[pl.BlockSpec((tq, w), lambda b, j: (b * nt + j, 0)),
                   pl.BlockSpec((1, 8, w), lambda b, j: (b, 0, 0))],
        out_shape=[jax.ShapeDtypeStruct((n, w), F32), jax.ShapeDtypeStruct((bsz, 8, w), F32)],
        compiler_params=_params("parallel", "arbitrary"),
    )(proj, proj, proj, proj, proj, wpad)


def _conv_sample_kernel(t_len, gb_ref, gc_ref, gh_ref, p0_ref, p1_ref, w_ref, o_ref, u_ref):
    u = gc_ref[...] * gh_ref[...]
    row = lax.broadcasted_iota(I32, u.shape, 0) % t_len
    u1 = jnp.where(row == 0, p1_ref[...], pltpu.roll(u, 1, 0))
    u2 = jnp.where(row == 0, p0_ref[...], jnp.where(row == 1, p1_ref[...], pltpu.roll(u, 2, 0)))
    y = w_ref[0:1] * u2 + w_ref[1:2] * u1 + w_ref[2:3] * u
    o_ref[...] = (gb_ref[...] * y).astype(o_ref.dtype)
    u_ref[...] = u


def conv_sample(proj, conv_w, buf, t_len):
    n = proj.shape[0]
    w = conv_w.shape[1]
    wpad = jnp.concatenate([conv_w, jnp.zeros((8 - CONV_W, w), F32)], axis=0)
    p0 = jnp.repeat(buf[:, 0], t_len, axis=0)
    p1 = jnp.repeat(buf[:, 1], t_len, axis=0)
    full = lambda cb: pl.BlockSpec((n, w), lambda i: (0, cb))
    return pl.pallas_call(
        functools.partial(_conv_sample_kernel, t_len),
        grid=(1,),
        in_specs=[full(4), full(5), full(6), full(0), full(0), pl.BlockSpec((8, w), lambda i: (0, 0))],
        out_specs=[full(0), full(0)],
        out_shape=[jax.ShapeDtypeStruct((n, w), F32), jax.ShapeDtypeStruct((n, w), F32)],
        compiler_params=_params("arbitrary"),
    )(proj, proj, proj, p0, p1, wpad)


def _float_key(s):
    b = pltpu.bitcast(s, I32)
    return b ^ (lax.shift_right_arithmetic(b, 31) & 0x7FFFFFFF)


def _dsa_prompt_kernel(n_sel, tk, q_ref, iq_ref, iw_ref, ike_ref, iko_ref, k_ref, v_ref, o_ref, key_ref, bias_ref):
    i = pl.program_id(1)
    tq = q_ref.shape[0]
    nck = ((i + 1) * tq + tk - 1) // tk

    @pl.when(pl.program_id(2) == 0)
    def _():
        _dsa_prompt_select(n_sel, tk, nck, i, iq_ref, iw_ref, ike_ref, iko_ref, key_ref, bias_ref)

    qh = _split(q_ref[...])

    def att_chunk(c, carry):
        m, l, acc = carry
        off = pl.multiple_of(c * tk, tk)
        s = _dot2_nt(qh, k_ref[pl.ds(off, tk), :]) * HEAD_DIM ** -0.5
        s = jnp.where(bias_ref[:, pl.ds(off, tk)] == 0.0, s, NEG_BIG)
        m_new = jnp.maximum(m, jnp.max(s, axis=-1, keepdims=True))
        alpha = jnp.exp(m - m_new)
        p = jnp.exp(s - m_new)
        l = alpha * l + jnp.sum(p, axis=-1, keepdims=True)
        acc = alpha * acc + _dot(p.astype(BF16), v_ref[pl.ds(off, tk), :].astype(BF16))
        return m_new, l, acc

    m, l, acc = lax.fori_loop(
        0, nck, att_chunk,
        (jnp.full((tq, 1), NEG_BIG, F32), jnp.zeros((tq, 1), F32), jnp.zeros((tq, HEAD_DIM), F32)))
    o_ref[...] = acc / l


def _dsa_prompt_select(n_sel, tk, nck, i, iq_ref, iw_ref, ike_ref, iko_ref, key_ref, bias_ref):
    tq = iq_ref.shape[0]
    row = i * tq + lax.broadcasted_iota(I32, (tq, tk), 0)
    col0 = lax.broadcasted_iota(I32, (tq, tk), 1)
    iw = iw_ref[...]

    def score_chunk(c, carry):
        off = pl.multiple_of(c * tk, tk)
        ke = _pair(ike_ref, pl.ds(off, tk))
        ko = _pair(iko_ref, pl.ds(off, tk))
        acc = jnp.zeros((tq, tk), F32)
        for hp in range(IDX_HEADS // 2):
            iq2 = _split(iq_ref[:, hp * LANES:(hp + 1) * LANES])
            s0 = _dot3_nt(iq2, ke) * IDX_DIM ** -0.5
            s1 = _dot3_nt(iq2, ko) * IDX_DIM ** -0.5
            acc = acc + jnp.maximum(s0, 0.0) * iw[:, IDX_DIM + 2 * hp:IDX_DIM + 2 * hp + 1]
            acc = acc + jnp.maximum(s1, 0.0) * iw[:, IDX_DIM + 2 * hp + 1:IDX_DIM + 2 * hp + 2]
        acc = acc * IDX_HEADS ** -0.5
        acc = jnp.where(off + col0 <= row, acc, -jnp.inf)
        key_ref[:, pl.ds(off, tk)] = _float_key(acc)
        return carry

    lax.fori_loop(0, nck, score_chunk, 0)

    def count_ge(cand):
        def body(c, acc):
            off = pl.multiple_of(c * tk, tk)
            m = jnp.where(key_ref[:, pl.ds(off, tk)] >= cand, 1.0, 0.0)
            for q in range(tk // LANES):
                acc = acc + m[:, q * LANES:(q + 1) * LANES]
            return acc
        acc = lax.fori_loop(0, nck, body, jnp.zeros((tq, LANES), F32))
        return jnp.sum(acc, axis=-1, keepdims=True)

    def bit_step(t, cur):
        cand = cur + lax.shift_left(jnp.int32(1), 31 - t)
        return jnp.where(count_ge(cand) >= n_sel, cand, cur)

    thr = lax.fori_loop(0, 32, bit_step, jnp.full((tq, 1), INT_MIN, I32))

    def bias_chunk(c, carry):
        off = pl.multiple_of(c * tk, tk)
        sel = (key_ref[:, pl.ds(off, tk)] >= thr) & (off + col0 <= row)
        bias_ref[:, pl.ds(off, tk)] = jnp.where(sel, 0.0, NEG_BIG)
        return carry

    lax.fori_loop(0, nck, bias_chunk, 0)


def dsa_prompt(proj, ik_e, ik_o, n_heads, bsz, t, tq, tk):
    n = proj.shape[0]
    nt = t // tq
    n_sel = min(IDX_TOPK, t // 4)
    return pl.pallas_call(
        functools.partial(_dsa_prompt_kernel, n_sel, tk),
        grid=(bsz, nt, n_heads),
        in_specs=[pl.BlockSpec((tq, HEAD_DIM), lambda b, i, h: (b * nt + i, h)),
                  pl.BlockSpec((tq, IDX_HEADS * IDX_DIM), lambda b, i, h: (b * nt + i, 3)),
                  pl.BlockSpec((tq, LANES), lambda b, i, h: (b * nt + i, 7 * n_heads)),
                  pl.BlockSpec((2, t, LANES), lambda b, i, h: (0, b, 0)),
                  pl.BlockSpec((2, t, LANES), lambda b, i, h: (0, b, 0)),
                  pl.BlockSpec((t, HEAD_DIM), lambda b, i, h: (b, n_heads + h)),
                  pl.BlockSpec((t, HEAD_DIM), lambda b, i, h: (b, 2 * n_heads + h))],
        out_specs=pl.BlockSpec((tq, HEAD_DIM), lambda b, i, h: (b * nt + i, h)),
        out_shape=jax.ShapeDtypeStruct((n, n_heads * HEAD_DIM), F32),
        scratch_shapes=[pltpu.VMEM((tq, t), I32), pltpu.VMEM((tq, t), F32)],
        compiler_params=_params("parallel", "arbitrary", "arbitrary"),
    )(proj, proj, proj, ik_e, ik_o, proj, proj)


def s5_weights(log_dt, a_re, a_im, b_re, b_im, c_re, c_im, chunk):
    hp = lax.Precision.HIGHEST
    g, p = a_re.shape
    dt = jnp.exp(log_dt.astype(F32))[:, None]
    lr, li = a_re.astype(F32), a_im.astype(F32)
    mag = jnp.exp(lr * dt)
    ab_re, ab_im = mag * jnp.cos(li * dt), mag * jnp.sin(li * dt)
    den = lr * lr + li * li
    f_re = ((ab_re - 1.0) * lr + ab_im * li) / den
    f_im = (ab_im * lr - (ab_re - 1.0) * li) / den
    br, bi = b_re.astype(F32), b_im.astype(F32)
    bb_re = f_re[..., None] * br - f_im[..., None] * bi
    bb_im = f_re[..., None] * bi + f_im[..., None] * br
    pr, pi = [jnp.ones_like(ab_re)], [jnp.zeros_like(ab_im)]
    for _ in range(chunk):
        pr.append(pr[-1] * ab_re - pi[-1] * ab_im)
        pi.append(pr[-2] * ab_im + pi[-1] * ab_re)
    pw_re, pw_im = jnp.stack(pr), jnp.stack(pi)
    cr, ci = c_re.astype(F32), c_im.astype(F32)
    rev_re = jnp.stack(pr[chunk - 1::-1])
    rev_im = jnp.stack(pi[chunk - 1::-1])
    w_re = rev_re[..., None] * bb_re[None] - rev_im[..., None] * bb_im[None]
    w_im = rev_re[..., None] * bb_im[None] + rev_im[..., None] * bb_re[None]
    z_re = cr[None] * pw_re[1:, :, None, :] - ci[None] * pw_im[1:, :, None, :]
    z_im = cr[None] * pw_im[1:, :, None, :] + ci[None] * pw_re[1:, :, None, :]
    ab_b_re = pw_re[:chunk, :, :, None] * bb_re[None] - pw_im[:chunk, :, :, None] * bb_im[None]
    ab_b_im = pw_re[:chunk, :, :, None] * bb_im[None] + pw_im[:chunk, :, :, None] * bb_re[None]
    k_tap = (jnp.einsum('gop,tgpi->tgoi', cr, ab_b_re, precision=hp)
             - jnp.einsum('gop,tgpi->tgoi', ci, ab_b_im, precision=hp))

    nblk = g * S5_GROUP // LANES
    gpb = g // nblk
    eye = jnp.eye(gpb, dtype=F32)

    def blockdiag(m, rows_last):
        c_, _, r_, k_ = m.shape
        m5 = m.reshape(c_, nblk, gpb, r_, k_)
        out = m5[:, :, :, :, None, :] * eye[None, None, :, None, :, None]
        return out.transpose(1, 0, 2, 3, 4, 5).reshape(nblk, c_, gpb * r_, gpb * k_)

    w_re_b = _split_stack(blockdiag(jnp.swapaxes(w_re, 2, 3), None))
    w_im_b = _split_stack(blockdiag(jnp.swapaxes(w_im, 2, 3), None))
    v_re_b = _split_stack(blockdiag(jnp.swapaxes(z_re, 2, 3), None))
    v_im_b = _split_stack(blockdiag(jnp.swapaxes(-z_im, 2, 3), None))
    k_b = _split_stack(blockdiag(jnp.swapaxes(k_tap, 2, 3), None))
    al_re = pw_re[chunk].reshape(nblk, 1, gpb * p)
    al_im = pw_im[chunk].reshape(nblk, 1, gpb * p)
    return w_re_b, w_im_b, v_re_b, v_im_b, k_b, al_re, al_im


def _s5_kernel(chunk, scan, u_ref, h0r_ref, h0i_ref, wre_ref, wim_ref, vre_ref, vim_ref, k_ref, ar_ref, ai_ref,
               d_ref, y_ref, hr_out, hi_out, hr_s, hi_s, sr_s, si_s):
    rc = sr_s.shape[0]
    xs = [u_ref[pl.ds(tok, rc, stride=chunk), :] for tok in range(chunk)]
    xb = [_split(x) for x in xs]
    sre = _dot3(xb[0], _pair(wre_ref, 0, 0))
    sim = _dot3(xb[0], _pair(wim_ref, 0, 0))
    for tok in range(1, chunk):
        sre = sre + _dot3(xb[tok], _pair(wre_ref, 0, tok))
        sim = sim + _dot3(xb[tok], _pair(wim_ref, 0, tok))
    ar, ai = ar_ref[0], ai_ref[0]
    if scan:
        @pl.when(pl.program_id(2) == 0)
        def _():
            hr_s[...] = h0r_ref[0, 0]
            hi_s[...] = h0i_ref[0, 0]

        sr_s[...] = sre
        si_s[...] = sim

        def body(c, carry):
            hr, hi = carry
            sr = sr_s[pl.ds(c, 1), :]
            si = si_s[pl.ds(c, 1), :]
            sr_s[pl.ds(c, 1), :] = hr
            si_s[pl.ds(c, 1), :] = hi
            return ar * hr - ai * hi + sr, ar * hi + ai * hr + si

        hr, hi = lax.fori_loop(0, rc, body, (hr_s[...], hi_s[...]))
        hr_s[...] = hr
        hi_s[...] = hi
        hr_out[0, 0] = hr
        hi_out[0, 0] = hi
        h_re, h_im = sr_s[...], si_s[...]
    else:
        h_re, h_im = h0r_ref[0], h0i_ref[0]
        hr_out[0] = ar * h_re - ai * h_im + sre
        hi_out[0] = ar * h_im + ai * h_re + sim
    hb_re, hb_im = _split(h_re), _split(h_im)
    for tok in range(chunk):
        acc = _dot3(hb_re, _pair(vre_ref, 0, tok)) + _dot3(hb_im, _pair(vim_ref, 0, tok))
        for s in range(tok + 1):
            acc = acc + _dot3(xb[s], _pair(k_ref, 0, tok - s))
        acc = acc + d_ref[0] * xs[tok]
        y_ref[pl.ds(tok, rc, stride=chunk), :] = acc


def s5_prompt(proj, h0r, h0i, wts, d_skip, bsz, t, chunk, rc):
    w_re, w_im, v_re, v_im, k_b, al_re, al_im = wts
    n = proj.shape[0]
    nblk = w_re.shape[1]
    sw = w_re.shape[4]
    cw = nblk * LANES
    nt = t // (rc * chunk)
    wspec = lambda a: pl.BlockSpec((2, 1) + a.shape[2:], lambda k, b, j: (0, k) + (0,) * (a.ndim - 2))
    aspec = lambda a: pl.BlockSpec((1,) + a.shape[1:], lambda k, b, j: (k,) + (0,) * (a.ndim - 1))
    hspec = pl.BlockSpec((1, 1, 1, sw), lambda k, b, j: (k, b, 0, 0))
    return pl.pallas_call(
        functools.partial(_s5_kernel, chunk, True),
        grid=(nblk, bsz, nt),
        in_specs=[pl.BlockSpec((rc * chunk, LANES), lambda k, b, j: (b * nt + j, k)),
                  hspec, hspec,
                  wspec(w_re), wspec(w_im), wspec(v_re), wspec(v_im), wspec(k_b), aspec(al_re), aspec(al_im),
                  pl.BlockSpec((1, 1, LANES), lambda k, b, j: (k, 0, 0))],
        out_specs=[pl.BlockSpec((rc * chunk, LANES), lambda k, b, j: (b * nt + j, k)), hspec, hspec],
        out_shape=[jax.ShapeDtypeStruct((n, cw), F32),
                   jax.ShapeDtypeStruct((nblk, bsz, 1, sw), F32),
                   jax.ShapeDtypeStruct((nblk, bsz, 1, sw), F32)],
        scratch_shapes=[pltpu.VMEM((1, sw), F32), pltpu.VMEM((1, sw), F32),
                        pltpu.VMEM((rc, sw), F32), pltpu.VMEM((rc, sw), F32)],
        compiler_params=_params("parallel", "parallel", "arbitrary"),
    )(proj, h0r, h0i, w_re, w_im, v_re, v_im, k_b, al_re, al_im, d_skip.reshape(nblk, 1, LANES))


def s5_sample(proj, h0r, h0i, wts, d_skip, bsz, chunk):
    w_re, w_im, v_re, v_im, k_b, al_re, al_im = wts
    n = proj.shape[0]
    nblk = w_re.shape[1]
    sw = w_re.shape[4]
    cw = nblk * LANES
    wspec = lambda a: pl.BlockSpec((2, 1) + a.shape[2:], lambda k: (0, k) + (0,) * (a.ndim - 2))
    aspec = lambda a: pl.BlockSpec((1,) + a.shape[1:], lambda k: (k,) + (0,) * (a.ndim - 1))
    hspec = pl.BlockSpec((1, bsz, sw), lambda k: (k, 0, 0))
    return pl.pallas_call(
        functools.partial(_s5_kernel, chunk, False),
        grid=(nblk,),
        in_specs=[pl.BlockSpec((n, LANES), lambda k: (0, k)),
                  hspec, hspec,
                  wspec(w_re), wspec(w_im), wspec(v_re), wspec(v_im), wspec(k_b), aspec(al_re), aspec(al_im),
                  pl.BlockSpec((1, 1, LANES), lambda k: (k, 0, 0))],
        out_specs=[pl.BlockSpec((n, LANES), lambda k: (0, k)), hspec, hspec],
        out_shape=[jax.ShapeDtypeStruct((n, cw), F32),
                   jax.ShapeDtypeStruct((nblk, bsz, sw), F32),
                   jax.ShapeDtypeStruct((nblk, bsz, sw), F32)],
        scratch_shapes=[pltpu.VMEM((1, sw), F32), pltpu.VMEM((1, sw), F32),
                        pltpu.VMEM((bsz, sw), F32), pltpu.VMEM((bsz, sw), F32)],
        compiler_params=_params("parallel"),
    )(proj, h0r, h0i, w_re, w_im, v_re, v_im, k_b, al_re, al_im, d_skip.reshape(nblk, 1, LANES))


def _glu_kernel(y_ref, w_ref, b_ref, o_ref):
    z = jax.nn.gelu(y_ref[...], approximate=True)
    o_ref[...] = z * jax.nn.sigmoid(_dot3(z, _pair(w_ref)) + b_ref[...])


def glu(y, w, b, tm):
    n, cw = y.shape
    return pl.pallas_call(
        _glu_kernel,
        grid=(n // tm,),
        in_specs=[pl.BlockSpec((tm, cw), lambda i: (i, 0)),
                  pl.BlockSpec((2, cw, cw), lambda i: (0, 0, 0)),
                  pl.BlockSpec((1, cw), lambda i: (0, 0))],
        out_specs=pl.BlockSpec((tm, cw), lambda i: (i, 0)),
        out_shape=jax.ShapeDtypeStruct((n, cw), F32),
        compiler_params=_params("parallel"),
    )(y, _split_stack(w), b.reshape(1, cw))


def _moba_prompt_kernel(nb, q_ref, k_ref, v_ref, o_ref, kmean_ref):
    i = pl.program_id(2)
    tq = q_ref.shape[0]

    @pl.when(i == 0)
    def _():
        kmean_ref[...] = jnp.zeros_like(kmean_ref)
        for n in range(nb):
            kmean_ref[n:n + 1, :] = jnp.mean(k_ref[n * MOBA_BLOCK:(n + 1) * MOBA_BLOCK, :], axis=0, keepdims=True)

    qb = _split(q_ref[...])
    gate = _dot3_nt(qb, kmean_ref[...])
    blk = lax.broadcasted_iota(I32, gate.shape, 1)
    gate = jnp.where(blk < i, gate, -jnp.inf)
    sel = jnp.zeros(gate.shape, F32)
    for _ in range(MOBA_TOPK):
        top = jnp.max(gate, axis=-1, keepdims=True)
        first = jnp.min(jnp.where(gate == top, blk, nb + LANES), axis=-1, keepdims=True)
        pick = (blk == first) & (first < i)
        sel = jnp.where(pick, 1.0, sel)
        gate = jnp.where(blk == first, -jnp.inf, gate)

    tk = 2 * MOBA_BLOCK
    row = lax.broadcasted_iota(I32, (tq, tk), 0)
    col = lax.broadcasted_iota(I32, (tq, tk), 1)
    second = col >= MOBA_BLOCK
    causal = jnp.where(jnp.where(second, col - MOBA_BLOCK, col) <= row, 1.0, 0.0)

    def att_block(c, carry):
        m, l, acc = carry
        off = pl.multiple_of(c * tk, tk)
        s = _dot2_nt(qb, k_ref[pl.ds(off, tk), :]) * HEAD_DIM ** -0.5
        picked0 = jnp.max(jnp.where(blk == 2 * c, sel, 0.0), axis=-1, keepdims=True)
        picked1 = jnp.max(jnp.where(blk == 2 * c + 1, sel, 0.0), axis=-1, keepdims=True)
        own = jnp.where(second, 2 * c + 1, 2 * c) == i
        valid = jnp.where(own, causal, jnp.where(second, picked1, picked0)) > 0.0
        s = jnp.where(valid, s, NEG_BIG)
        m_new = jnp.maximum(m, jnp.max(s, axis=-1, keepdims=True))
        alpha = jnp.exp(m - m_new)
        p = jnp.where(valid, jnp.exp(s - m_new), 0.0)
        l = alpha * l + jnp.sum(p, axis=-1, keepdims=True)
        acc = alpha * acc + _dot(p.astype(BF16), v_ref[pl.ds(off, tk), :].astype(BF16))
        return m_new, l, acc

    m, l, acc = lax.fori_loop(
        0, i // 2 + 1, att_block,
        (jnp.full((tq, 1), NEG_BIG, F32), jnp.zeros((tq, 1), F32), jnp.zeros((tq, HEAD_DIM), F32)))
    o_ref[...] = acc / l


def moba_prompt(proj, bsz, t):
    n = proj.shape[0]
    n_heads = proj.shape[1] // 4 // HEAD_DIM
    nb = t // MOBA_BLOCK
    assert t % (2 * MOBA_BLOCK) == 0
    nbp = -(-nb // 8) * 8
    return pl.pallas_call(
        functools.partial(_moba_prompt_kernel, nb),
        grid=(bsz, n_heads, nb),
        in_specs=[pl.BlockSpec((MOBA_BLOCK, HEAD_DIM), lambda b, h, i: (b * nb + i, n_heads + h)),
                  pl.BlockSpec((t, HEAD_DIM), lambda b, h, i: (b, 2 * n_heads + h)),
                  pl.BlockSpec((t, HEAD_DIM), lambda b, h, i: (b, 3 * n_heads + h))],
        out_specs=pl.BlockSpec((MOBA_BLOCK, HEAD_DIM), lambda b, h, i: (b * nb + i, h)),
        out_shape=jax.ShapeDtypeStruct((n, n_heads * HEAD_DIM), F32),
        scratch_shapes=[pltpu.VMEM((nbp, HEAD_DIM), F32)],
        compiler_params=_params("parallel", "parallel", "arbitrary"),
    )(proj, proj, proj)


def _router_kernel(x_ref, g_ref, sc_ref, sh_ref, w_ref, b_ref, hn_ref, eid_ref, rank_ref, wt_ref, cnt_ref, carry_ref):
    i = pl.program_id(0)
    tm = x_ref.shape[0]

    @pl.when(i == 0)
    def _():
        carry_ref[...] = jnp.zeros_like(carry_ref)

    hn = _norm_mod(x_ref[...], g_ref[...], sc_ref[0, 0], sh_ref[0, 0])
    hn_ref[...] = hn
    logit = _dot3(hn, w_ref[...]) + b_ref[...]
    lane = lax.broadcasted_iota(I32, logit.shape, 1)
    gmask = lane < N_GROUPS
    gl = jnp.where(gmask, logit, -jnp.inf)
    gmax = jnp.max(gl, axis=-1, keepdims=True)
    gsel = jnp.min(jnp.where(gl == gmax, lane, LANES), axis=-1, keepdims=True)
    g_gate = 1.0 / jnp.sum(jnp.where(gmask, jnp.exp(gl - gmax), 0.0), axis=-1, keepdims=True)
    lo = N_GROUPS + gsel * EXP_PER_GROUP
    emask = (lane >= lo) & (lane < lo + EXP_PER_GROUP)
    el = jnp.where(emask, logit, -jnp.inf)
    v1 = jnp.max(el, axis=-1, keepdims=True)
    i1 = jnp.min(jnp.where(el == v1, lane, LANES), axis=-1, keepdims=True)
    el2 = jnp.where(lane == i1, -jnp.inf, el)
    v2 = jnp.max(el2, axis=-1, keepdims=True)
    i2 = jnp.min(jnp.where(el2 == v2, lane, LANES), axis=-1, keepdims=True)
    e21 = jnp.exp(v2 - v1)
    w1 = g_gate / (1.0 + e21)
    w2 = w1 * e21
    oh1 = jnp.where(lane == i1, 1.0, 0.0)
    oh2 = jnp.where(lane == i2, 1.0, 0.0)
    oh = oh1 + oh2
    r_i = lax.broadcasted_iota(I32, (tm, tm), 0)
    c_i = lax.broadcasted_iota(I32, (tm, tm), 1)
    tri = jnp.where(c_i < r_i, 1.0, 0.0).astype(BF16)
    before = _dot(tri, oh.astype(BF16)) + carry_ref[0:1, :]
    rank1 = jnp.sum(oh1 * before, axis=-1, keepdims=True).astype(I32)
    rank2 = jnp.sum(oh2 * before, axis=-1, keepdims=True).astype(I32)
    eid_ref[...] = jnp.where(lane == 0, i1 - N_GROUPS, jnp.where(lane == 1, i2 - N_GROUPS, 0))
    rank_ref[...] = jnp.where(lane == 0, rank1, jnp.where(lane == 1, rank2, 0))
    wt_ref[...] = jnp.where(lane == 0, w1, jnp.where(lane == 1, w2, 0.0))
    carry_ref[0:1, :] = carry_ref[0:1, :] + jnp.sum(oh, axis=0, keepdims=True)
    cnt_ref[...] = carry_ref[...]


def router(x, g, mod6, w_r, b_r, tm, rows_per_mod):
    n, d = x.shape
    rm = mod6.shape[2]
    tok = lambda width, dt: (pl.BlockSpec((tm, width), lambda i: (i, 0)), jax.ShapeDtypeStruct((n, width), dt))
    specs = [tok(d, F32), tok(LANES, I32), tok(LANES, I32), tok(LANES, F32),
             (pl.BlockSpec((8, LANES), lambda i: (0, 0)), jax.ShapeDtypeStruct((8, LANES), F32))]
    return pl.pallas_call(
        _router_kernel,
        grid=(n // tm,),
        in_specs=[pl.BlockSpec((tm, d), lambda i: (i, 0)),
                  pl.BlockSpec((1, d), lambda i: (0, 0)),
                  _mod_spec(4, tm, rows_per_mod, rm, d, 1),
                  _mod_spec(3, tm, rows_per_mod, rm, d, 1),
                  pl.BlockSpec((d, LANES), lambda i: (0, 0)),
                  pl.BlockSpec((1, LANES), lambda i: (0, 0))],
        out_specs=[s for s, _ in specs],
        out_shape=[o for _, o in specs],
        scratch_shapes=[pltpu.VMEM((8, LANES), F32)],
        compiler_params=_params("arbitrary"),
    )(x, g.reshape(1, d), mod6, mod6, w_r, b_r)


def _expert_kernel(te_ref, tv_ref, src_ref, dst_ref, hn_hbm, w1_ref, w3_ref, w2_ref, y_hbm, xbuf, ybuf, gsem, ssem):
    j = pl.program_id(0)
    nj = pl.num_programs(0)
    tm = ybuf.shape[0]

    def gather_rows(tile, op):
        slot = tile % 2

        def body(r, c):
            op(pltpu.make_async_copy(hn_hbm.at[pl.ds(src_ref[tile * tm + r], 1)], xbuf.at[slot, pl.ds(r, 1)],
                                     gsem.at[slot]))
            return c
        lax.fori_loop(0, tm, body, 0, unroll=8)

    def scatter_rows(tile, op):
        def body(r, c):
            row = dst_ref[tile * tm + r]

            @pl.when(row >= 0)
            def _():
                op(pltpu.make_async_copy(ybuf.at[pl.ds(r, 1)], y_hbm.at[pl.ds(row, 1)], ssem))
            return c
        lax.fori_loop(0, tm, body, 0, unroll=8)

    start = lambda c: c.start()
    wait = lambda c: c.wait()

    @pl.when((j == 0) & (tv_ref[0] > 0))
    def _():
        gather_rows(0, start)

    nxt = jnp.minimum(j + 1, nj - 1)

    @pl.when((j + 1 < nj) & (tv_ref[nxt] > 0))
    def _():
        gather_rows(j + 1, start)

    prev = jnp.maximum(j - 1, 0)

    @pl.when((j > 0) & (tv_ref[prev] > 0))
    def _():
        scatter_rows(j - 1, wait)

    @pl.when(tv_ref[j] > 0)
    def _():
        gather_rows(j, wait)
        xb = _split(xbuf[j % 2])
        a = _dot3(xb, w1_ref[0, 0])
        b = _dot3(xb, w3_ref[0, 0])
        h = (a * jax.nn.sigmoid(a)) * b
        ybuf[...] = _dot3(h, w2_ref[0, 0])
        scatter_rows(j, start)

        @pl.when(j == nj - 1)
        def _():
            scatter_rows(j, wait)


def experts(hn, tile_e, tile_v, src, dst, w1, w3, w2, layer, tm, n_out):
    n_tiles = tile_e.shape[0]
    d = hn.shape[1]
    ff = w1.shape[3]
    grid_spec = pltpu.PrefetchScalarGridSpec(
        num_scalar_prefetch=4,
        grid=(n_tiles,),
        in_specs=[pl.BlockSpec(memory_space=pl.ANY),
                  pl.BlockSpec((1, 1, d, ff), lambda j, te, tv, s, t: (layer, te[j], 0, 0)),
                  pl.BlockSpec((1, 1, d, ff), lambda j, te, tv, s, t: (layer, te[j], 0, 0)),
                  pl.BlockSpec((1, 1, ff, d), lambda j, te, tv, s, t: (layer, te[j], 0, 0))],
        out_specs=pl.BlockSpec(memory_space=pl.ANY),
        scratch_shapes=[pltpu.VMEM((2, tm, d), F32), pltpu.VMEM((tm, d), F32),
                        pltpu.SemaphoreType.DMA((2,)), pltpu.SemaphoreType.DMA(())],
    )
    return pl.pallas_call(
        _expert_kernel,
        grid_spec=grid_spec,
        out_shape=jax.ShapeDtypeStruct((n_out, d), F32),
        compiler_params=_params("arbitrary"),
    )(tile_e, tile_v, src, dst, hn, w1, w3, w2)


def _combine_kernel(x_ref, g_ref, wt_ref, y0_ref, y1_ref, o_ref):
    wt = wt_ref[...]
    o_ref[...] = x_ref[...] + g_ref[0, 0] * (wt[:, 0:1] * y0_ref[...] + wt[:, 1:2] * y1_ref[...])


def combine(x, mod6, wt, y, tm, rows_per_mod):
    n, d = x.shape
    rm = mod6.shape[2]
    nt = n // tm
    return pl.pallas_call(
        _combine_kernel,
        grid=(nt,),
        in_specs=[pl.BlockSpec((tm, d), lambda i: (i, 0)),
                  _mod_spec(5, tm, rows_per_mod, rm, d, 1),
                  pl.BlockSpec((tm, LANES), lambda i: (i, 0)),
                  pl.BlockSpec((tm, d), lambda i: (i, 0)),
                  pl.BlockSpec((tm, d), lambda i: (i + nt, 0))],
        out_specs=pl.BlockSpec((tm, d), lambda i: (i, 0)),
        out_shape=jax.ShapeDtypeStruct((n, d), F32),
        compiler_params=_params("parallel"),
    )(x, mod6, wt, y, y)


def hier_moe(x, g, mod6, w_r, b_r, w1, w3, w2, layer, tm, rows_per_mod, tm_e):
    n, d = x.shape
    hn, eid, rank, wt, cnt = router(x, g, mod6, w_r, b_r, tm, rows_per_mod)
    cnt = cnt[0, N_GROUPS:N_GROUPS + N_EXPERTS].astype(I32)
    padded = ((cnt + tm_e - 1) // tm_e) * tm_e
    ends = jnp.cumsum(padded)
    offs = ends - padded
    e2, r2 = eid[:, :MOE_TOPK], rank[:, :MOE_TOPK]
    pos = (offs[e2] + r2).reshape(-1)
    n_tiles = (n * MOE_TOPK) // tm_e + N_EXPERTS
    n_rows = n_tiles * tm_e
    tok_id = jnp.repeat(jnp.arange(n, dtype=I32), MOE_TOPK)
    pick_id = jnp.tile(jnp.arange(MOE_TOPK, dtype=I32), n)
    dst = jnp.full((n_rows,), -1, I32).at[pos].set(pick_id * n + tok_id)
    src = jnp.where(dst >= 0, dst % n, 0)
    starts = jnp.arange(n_tiles, dtype=I32) * tm_e
    tile_e = jnp.minimum(jnp.sum((starts[:, None] >= ends[None, :]).astype(I32), axis=1), N_EXPERTS - 1)
    tile_v = (starts < ends[-1]).astype(I32)
    n_out = n * MOE_TOPK
    y = experts(hn, tile_e, tile_v, src, dst, w1, w3, w2, layer, tm_e, n_out)
    return combine(x, mod6, wt, y, tm, rows_per_mod)


SAMPLE_ROWS = 8


def _dsa_select_kernel(layer, n_pages, n_sel, t_new, pt_ref, iq_ref, iw_ref, ikn_ref, kidx_hbm,
                       idx_ref, cnt_ref, mnew_ref, kbuf, score_ref, key_ref, pfx_ref, sem):
    b = pl.program_id(0)
    n_past = n_pages * PAGE_SIZE
    tp = SAMPLE_ROWS
    ck = min(2048, n_past)

    def page_copy(pg):
        return pltpu.make_async_copy(kidx_hbm.at[layer, pt_ref[b * n_pages + pg]],
                                     kbuf.at[pl.ds(pg * PAGE_SIZE, PAGE_SIZE)], sem)

    def start(pg, c):
        page_copy(pg).start()
        return c

    def wait(pg, c):
        page_copy(pg).wait()
        return c

    lax.fori_loop(0, n_pages, start, 0)
    lax.fori_loop(0, n_pages, wait, 0)

    iq = _split(iq_ref[0])
    iw = iw_ref[0]

    def head_sum(s):
        s = jnp.maximum(s * IDX_DIM ** -0.5, 0.0) * iw
        rows = [jnp.sum(s[t * IDX_HEADS:(t + 1) * IDX_HEADS], axis=0, keepdims=True) for t in range(tp)]
        return jnp.concatenate(rows, axis=0) * IDX_HEADS ** -0.5

    def score_chunk(c, carry):
        off = pl.multiple_of(c * ck, ck)
        sc = head_sum(_dot3_nt(iq, kbuf[pl.ds(off, ck), :]))
        score_ref[:, pl.ds(off, ck)] = sc
        key_ref[:, pl.ds(off, ck)] = _float_key(sc)
        return carry

    lax.fori_loop(0, n_past // ck, score_chunk, 0)
    ikn = jnp.concatenate([ikn_ref[0], jnp.zeros((LANES - tp, IDX_DIM), F32)], axis=0)
    s_new = head_sum(_dot3_nt(iq, ikn))
    trow = lax.broadcasted_iota(I32, (tp, LANES), 0)
    jcol = lax.broadcasted_iota(I32, (tp, LANES), 1)
    s_new = jnp.where((jcol <= trow) & (jcol < t_new), s_new, -jnp.inf)
    key_new = _float_key(s_new)

    def count_ge(cand):
        def body(c, acc):
            off = pl.multiple_of(c * ck, ck)
            m = jnp.where(key_ref[:, pl.ds(off, ck)] >= cand, 1.0, 0.0)
            for q in range(ck // LANES):
                acc = acc + m[:, q * LANES:(q + 1) * LANES]
            return acc
        acc = lax.fori_loop(0, n_past // ck, body, jnp.where(key_new >= cand, 1.0, 0.0))
        return jnp.sum(acc, axis=-1, keepdims=True)

    def bit_step(t, cur):
        cand = cur + lax.shift_left(jnp.int32(1), 31 - t)
        return jnp.where(count_ge(cand) >= n_sel, cand, cur)

    thr = lax.fori_loop(0, 32, bit_step, jnp.full((tp, 1), INT_MIN, I32))
    mnew_ref[0] = jnp.where((key_new >= thr) & (s_new > -jnp.inf), 1.0, 0.0)

    pb = 256
    r_i = lax.broadcasted_iota(I32, (pb, pb), 0)
    c_i = lax.broadcasted_iota(I32, (pb, pb), 1)
    upper = jnp.where(r_i < c_i, 1.0, 0.0).astype(BF16)

    def prefix_block(c, run):
        off = pl.multiple_of(c * pb, pb)
        m = jnp.where(key_ref[:, pl.ds(off, pb)] >= thr, 1.0, 0.0)
        before = _dot(m.astype(BF16), upper) + run
        pfx_ref[:, pl.ds(off, pb)] = jnp.where(m > 0.0, before, -1.0)
        return run + jnp.sum(m, axis=-1, keepdims=True)

    total = lax.fori_loop(0, n_past // pb, prefix_block, jnp.zeros((tp, 1), F32))
    cnt_ref[0] = jnp.broadcast_to(jnp.minimum(total, float(n_sel)), (tp, LANES))

    slot = lax.broadcasted_iota(I32, (n_sel, ck), 0).astype(F32)
    kpos = lax.broadcasted_iota(I32, (n_sel, ck), 1).astype(F32)
    for t in range(t_new):
        def slot_chunk(c, acc, t=t):
            off = pl.multiple_of(c * ck, ck)
            pf = pfx_ref[t:t + 1, pl.ds(off, ck)]
            hit = jnp.where(pf == slot, kpos + jnp.asarray(c * ck, F32), 0.0)
            return acc + jnp.sum(hit, axis=-1, keepdims=True)
        pos = lax.fori_loop(0, n_past // ck, slot_chunk, jnp.zeros((n_sel, 1), F32))
        idx_ref[0, t] = pos.astype(I32)


def dsa_select(iq, iw, ik_new, pool_kidx, page_table, layer, t_new):
    bsz = iq.shape[0]
    n_pages = page_table.shape[1]
    n_past = n_pages * PAGE_SIZE
    n_sel = min(IDX_TOPK, (n_past + t_new) // 4)
    tp = SAMPLE_ROWS
    grid_spec = pltpu.PrefetchScalarGridSpec(
        num_scalar_prefetch=1,
        grid=(bsz,),
        in_specs=[pl.BlockSpec((1, tp * IDX_HEADS, IDX_DIM), lambda b, pt: (b, 0, 0)),
                  pl.BlockSpec((1, tp * IDX_HEADS, 1), lambda b, pt: (b, 0, 0)),
                  pl.BlockSpec((1, tp, IDX_DIM), lambda b, pt: (b, 0, 0)),
                  pl.BlockSpec(memory_space=pl.ANY)],
        out_specs=[pl.BlockSpec((1, t_new, n_sel, 1), lambda b, pt: (b, 0, 0, 0)),
                   pl.BlockSpec((1, tp, LANES), lambda b, pt: (b, 0, 0)),
                   pl.BlockSpec((1, tp, LANES), lambda b, pt: (b, 0, 0))],
        scratch_shapes=[pltpu.VMEM((n_past, IDX_DIM), F32), pltpu.VMEM((tp, n_past), F32),
                        pltpu.VMEM((tp, n_past), I32), pltpu.VMEM((tp, n_past), F32),
                        pltpu.SemaphoreType.DMA(())],
    )
    return pl.pallas_call(
        functools.partial(_dsa_select_kernel, layer, n_pages, n_sel, t_new),
        grid_spec=grid_spec,
        out_shape=[jax.ShapeDtypeStruct((bsz, t_new, n_sel, 1), I32),
                   jax.ShapeDtypeStruct((bsz, tp, LANES), F32),
                   jax.ShapeDtypeStruct((bsz, tp, LANES), F32)],
        compiler_params=_params("arbitrary"),
    )(page_table.reshape(-1), iq, iw, ik_new, pool_kidx)


def _dsa_gather_kernel(layer, n_pages, n_sel, t_new, pt_ref, idx_ref, cnt_ref, mnew_ref, q_ref, kn_ref, vn_ref,
                       k_hbm, v_hbm, o_ref, kbuf, vbuf, ksem, vsem):
    b = pl.program_id(0)
    t = pl.program_id(1)
    base = (b * t_new + t) * n_sel

    def copies(j):
        pos = idx_ref[base + j]
        page = pt_ref[b * n_pages + pos // PAGE_SIZE]
        row = pos % PAGE_SIZE
        return (pltpu.make_async_copy(k_hbm.at[layer, page, pl.ds(row, 1)], kbuf.at[pl.ds(j, 1)], ksem),
                pltpu.make_async_copy(v_hbm.at[layer, page, pl.ds(row, 1)], vbuf.at[pl.ds(j, 1)], vsem))

    def start(j, c):
        ck, cv = copies(j)
        ck.start()
        cv.start()
        return c

    def wait(j, c):
        ck, cv = copies(j)
        ck.wait()
        cv.wait()
        return c

    lax.fori_loop(0, n_sel, start, 0)
    lax.fori_loop(0, n_sel, wait, 0)

    q = q_ref[0]
    scale = HEAD_DIM ** -0.5
    s = jnp.sum(kbuf[...] * q, axis=-1, keepdims=True) * scale
    slot = lax.broadcasted_iota(I32, s.shape, 0)
    s = jnp.where(slot < cnt_ref[b * t_new + t], s, NEG_BIG)
    kn = kn_ref[0]
    s_n = jnp.sum(kn * q, axis=-1, keepdims=True) * scale
    tn = lax.broadcasted_iota(I32, s_n.shape, 0)
    ok = jnp.zeros(s_n.shape, F32)
    for j in range(t_new):
        ok = jnp.where(tn == j, mnew_ref[(b * t_new + t) * t_new + j].astype(F32), ok)
    s_n = jnp.where(ok > 0.0, s_n, NEG_BIG)
    m = jnp.maximum(jnp.max(s, axis=0, keepdims=True), jnp.max(s_n, axis=0, keepdims=True))
    p = jnp.exp(s - m)
    p_n = jnp.exp(s_n - m)
    l = jnp.sum(p, axis=0, keepdims=True) + jnp.sum(p_n, axis=0, keepdims=True)
    acc = jnp.sum(p * vbuf[...], axis=0, keepdims=True) + jnp.sum(p_n * vn_ref[0], axis=0, keepdims=True)
    o_ref[0] = acc / l


def dsa_gather(q, k_new, v_new, idx, cnt, mnew, pool_k, pool_v, page_table, layer):
    bsz, t_new, n_heads, dh = q.shape
    n_pages = page_table.shape[1]
    n_sel = idx.shape[0] // (bsz * t_new)
    grid_spec = pltpu.PrefetchScalarGridSpec(
        num_scalar_prefetch=4,
        grid=(bsz, t_new),
        in_specs=[pl.BlockSpec((1, 1, n_heads, dh), lambda b, t, *_: (b, t, 0, 0)),
                  pl.BlockSpec((1, t_new, n_heads, dh), lambda b, t, *_: (b, 0, 0, 0)),
                  pl.BlockSpec((1, t_new, n_heads, dh), lambda b, t, *_: (b, 0, 0, 0)),
                  pl.BlockSpec(memory_space=pl.ANY), pl.BlockSpec(memory_space=pl.ANY)],
        out_specs=pl.BlockSpec((1, 1, n_heads, dh), lambda b, t, *_: (b, t, 0, 0)),
        scratch_shapes=[pltpu.VMEM((n_sel, n_heads, dh), F32), pltpu.VMEM((n_sel, n_heads, dh), F32),
                        pltpu.SemaphoreType.DMA(()), pltpu.SemaphoreType.DMA(())],
    )
    return pl.pallas_call(
        functools.partial(_dsa_gather_kernel, layer, n_pages, n_sel, t_new),
        grid_spec=grid_spec,
        out_shape=jax.ShapeDtypeStruct((bsz, t_new, n_heads, dh), F32),
        compiler_params=_params("arbitrary", "arbitrary"),
    )(page_table.reshape(-1), idx, cnt, mnew, q, k_new, v_new, pool_k, pool_v)


def dsa_sample(proj, ikiw, pool_k, pool_v, pool_kidx, page_table, layer, bsz, t_new):
    aw = pool_k.shape[3] * pool_k.shape[4]
    n_heads = pool_k.shape[3]
    tp = SAMPLE_ROWS
    hs = (bsz, t_new, n_heads, HEAD_DIM)
    pad_t = lambda a: jnp.concatenate([a, jnp.zeros((bsz, tp - t_new) + a.shape[2:], a.dtype)], axis=1)
    iq = pad_t(proj[:, 3 * aw:3 * aw + IDX_HEADS * IDX_DIM].reshape(bsz, t_new, IDX_HEADS, IDX_DIM))
    iw = pad_t(ikiw[:, IDX_DIM:IDX_DIM + IDX_HEADS].reshape(bsz, t_new, IDX_HEADS))
    ik_new = pad_t(ikiw[:, :IDX_DIM].reshape(bsz, t_new, IDX_DIM))
    idx, cnt, mnew = dsa_select(iq.reshape(bsz, tp * IDX_HEADS, IDX_DIM), iw.reshape(bsz, tp * IDX_HEADS, 1),
                                ik_new, pool_kidx, page_table, layer, t_new)
    cnt = cnt[:, :t_new, 0].astype(I32).reshape(-1)
    mnew = mnew[:, :t_new, :t_new].astype(I32).reshape(-1)
    out = dsa_gather(proj[:, :aw].reshape(hs), proj[:, aw:2 * aw].reshape(hs), proj[:, 2 * aw:3 * aw].reshape(hs),
                     idx.reshape(-1), cnt, mnew, pool_k, pool_v, page_table, layer)
    return out.reshape(bsz * t_new, aw)


def _moba_mean_kernel(ppb, bps, pt_ref, *refs):
    pages, o_ref = refs[:ppb * bps], refs[ppb * bps]
    for s in range(bps):
        acc = jnp.sum(pages[s * ppb][0, 0], axis=0)
        for pg in pages[s * ppb + 1:(s + 1) * ppb]:
            acc = acc + jnp.sum(pg[0, 0], axis=0)
        o_ref[0, s] = acc * (1.0 / MOBA_BLOCK)


def moba_block_means(pool_k, page_table, layer):
    bsz, n_pages = page_table.shape
    _, _, page, n_heads, dh = pool_k.shape
    ppb = MOBA_BLOCK // page
    nbp = n_pages // ppb
    bps = 4 if nbp % 4 == 0 else 1

    def page_spec(j):
        return pl.BlockSpec((1, 1, page, n_heads, dh),
                            lambda b, n, pt: (layer, pt[b * n_pages + n * ppb * bps + j], 0, 0, 0))

    grid_spec = pltpu.PrefetchScalarGridSpec(
        num_scalar_prefetch=1,
        grid=(bsz, nbp // bps),
        in_specs=[page_spec(j) for j in range(ppb * bps)],
        out_specs=pl.BlockSpec((1, bps, n_heads, dh), lambda b, n, pt: (b, n, 0, 0)),
    )
    return pl.pallas_call(
        functools.partial(_moba_mean_kernel, ppb, bps),
        grid_spec=grid_spec,
        out_shape=jax.ShapeDtypeStruct((bsz, nbp, n_heads, dh), F32),
        compiler_params=_params("parallel", "arbitrary"),
    )(page_table.reshape(-1), *([pool_k] * (ppb * bps)))


def _moba_pick_kernel(n_top, km_ref, q_ref, sel_ref):
    km = km_ref[0]
    nbp = km.shape[0]
    t_new = q_ref.shape[1]
    for t in range(t_new):
        s = jnp.sum(km * q_ref[0, t:t + 1], axis=-1, keepdims=True)
        blk = lax.broadcasted_iota(I32, s.shape, 0)
        for j in range(n_top):
            top = jnp.max(s, axis=0, keepdims=True)
            first = jnp.min(jnp.where(s == top, blk, nbp), axis=0, keepdims=True)
            sel_ref[0, t, j] = first[0]
            s = jnp.where(blk == first, -jnp.inf, s)


def moba_pick(kmean, q, n_top):
    bsz, nbp, n_heads, dh = kmean.shape
    t_new = q.shape[1]
    return pl.pallas_call(
        functools.partial(_moba_pick_kernel, n_top),
        grid=(bsz,),
        in_specs=[pl.BlockSpec((1, nbp, n_heads, dh), lambda b: (b, 0, 0, 0)),
                  pl.BlockSpec((1, t_new, n_heads, dh), lambda b: (b, 0, 0, 0))],
        out_specs=pl.BlockSpec((1, t_new, n_top, n_heads, 1), lambda b: (b, 0, 0, 0, 0)),
        out_shape=jax.ShapeDtypeStruct((bsz, t_new, n_top, n_heads, 1), I32),
        compiler_params=_params("parallel"),
    )(kmean, q)


def _moba_gather_kernel(layer, n_pages, n_top, ppb, pt_ref, sel_ref, q_ref, kn_ref, vn_ref, k_hbm, v_hbm, o_ref,
                        kbuf, vbuf, ksem, vsem):
    b = pl.program_id(0)
    t = pl.program_id(1)
    t_new = pl.num_programs(1)
    n_heads = kbuf.shape[0]
    page = PAGE_SIZE
    copies = []
    for h in range(n_heads):
        for j in range(n_top):
            blk = sel_ref[((b * t_new + t) * n_top + j) * n_heads + h]
            for pg in range(ppb):
                phys = pt_ref[b * n_pages + blk * ppb + pg]
                dst = pl.ds((j * ppb + pg) * page, page)
                copies.append(pltpu.make_async_copy(k_hbm.at[layer, phys, :, h, :], kbuf.at[h, dst], ksem))
                copies.append(pltpu.make_async_copy(v_hbm.at[layer, phys, :, h, :], vbuf.at[h, dst], vsem))
    for c in copies:
        c.start()
    for c in copies:
        c.wait()

    q = q_ref[0, 0]
    scale = HEAD_DIM ** -0.5
    s = jnp.sum(kbuf[...] * q, axis=-1, keepdims=True) * scale
    s_n = jnp.sum(kn_ref[0] * q, axis=-1, keepdims=True) * scale
    tn = lax.broadcasted_iota(I32, s_n.shape, 1)
    s_n = jnp.where(tn <= t, s_n, NEG_BIG)
    m = jnp.maximum(jnp.max(s, axis=1, keepdims=True), jnp.max(s_n, axis=1, keepdims=True))
    p = jnp.exp(s - m)
    p_n = jnp.exp(s_n - m)
    l = jnp.sum(p, axis=1, keepdims=True) + jnp.sum(p_n, axis=1, keepdims=True)
    acc = jnp.sum(p * vbuf[...], axis=1, keepdims=True) + jnp.sum(p_n * vn_ref[0], axis=1, keepdims=True)
    o_ref[0, 0] = acc / l


def moba_gather(q, k_new, v_new, sel, pool_k, pool_v, page_table, layer, n_top):
    bsz, t_new, n_heads, _, dh = q.shape
    n_pages = page_table.shape[1]
    ppb = MOBA_BLOCK // PAGE_SIZE
    grid_spec = pltpu.PrefetchScalarGridSpec(
        num_scalar_prefetch=2,
        grid=(bsz, t_new),
        in_specs=[pl.BlockSpec((1, 1, n_heads, 1, dh), lambda b, t, *_: (b, t, 0, 0, 0)),
                  pl.BlockSpec((1, n_heads, t_new, dh), lambda b, t, *_: (b, 0, 0, 0)),
                  pl.BlockSpec((1, n_heads, t_new, dh), lambda b, t, *_: (b, 0, 0, 0)),
                  pl.BlockSpec(memory_space=pl.ANY), pl.BlockSpec(memory_space=pl.ANY)],
        out_specs=pl.BlockSpec((1, 1, n_heads, 1, dh), lambda b, t, *_: (b, t, 0, 0, 0)),
        scratch_shapes=[pltpu.VMEM((n_heads, n_top * MOBA_BLOCK, dh), F32),
                        pltpu.VMEM((n_heads, n_top * MOBA_BLOCK, dh), F32),
                        pltpu.SemaphoreType.DMA(()), pltpu.SemaphoreType.DMA(())],
    )
    return pl.pallas_call(
        functools.partial(_moba_gather_kernel, layer, n_pages, n_top, ppb),
        grid_spec=grid_spec,
        out_shape=jax.ShapeDtypeStruct((bsz, t_new, n_heads, 1, dh), F32),
        compiler_params=_params("arbitrary", "arbitrary"),
    )(page_table.reshape(-1), sel, q, k_new, v_new, pool_k, pool_v)


def moba_sample(proj, pool_k, pool_v, page_table, layer, bsz, t_new):
    n_heads = pool_k.shape[3]
    aw = n_heads * HEAD_DIM
    n_past = page_table.shape[1] * PAGE_SIZE
    assert n_past % MOBA_BLOCK == 0 and t_new <= MOBA_BLOCK
    n_top = min(MOBA_TOPK, n_past // MOBA_BLOCK)
    hs = (bsz, t_new, n_heads, HEAD_DIM)
    q = proj[:, aw:2 * aw].reshape(hs)
    k_new = proj[:, 2 * aw:3 * aw].reshape(hs).transpose(0, 2, 1, 3)
    v_new = proj[:, 3 * aw:4 * aw].reshape(hs).transpose(0, 2, 1, 3)
    kmean = moba_block_means(pool_k, page_table, layer)
    sel = moba_pick(kmean, q, n_top)
    out = moba_gather(q[:, :, :, None, :], k_new, v_new, sel.reshape(-1), pool_k, pool_v, page_table, layer, n_top)
    return out.reshape(bsz * t_new, aw)


def _ab_weight(w):
    d = w.shape[0]
    aw = (w.shape[1] - IDX_HEADS * IDX_DIM - IDX_DIM - IDX_HEADS) // 6
    main = 3 * aw + IDX_HEADS * IDX_DIM
    small = IDX_DIM + IDX_HEADS
    pad = 512 - small
    return _split_stack(jnp.concatenate([w[:, :main], w[:, main + small:], w[:, main:main + small],
                                         jnp.zeros((d, pad), w.dtype)], axis=1))


def _router_weight(wg, bg, we, be):
    d = wg.shape[0]
    pad = LANES - N_GROUPS - N_EXPERTS
    w = jnp.concatenate([wg, we, jnp.zeros((d, pad), F32)], axis=1)
    b = jnp.concatenate([bg, be, jnp.zeros((pad,), F32)]).reshape(1, LANES)
    return w, b


class _Tiles(NamedTuple):
    rows: int
    rows_per_mod: int
    expert_rows: int
    dsa_q: int = 256
    dsa_k: int = 512
    conv_rows: int = 512
    s5_chunk: int = 8
    s5_chunks: int = 64


def _tiles(prompt, n, t):
    if prompt:
        return _Tiles(rows=512, rows_per_mod=t, expert_rows=256)
    return _Tiles(rows=n, rows_per_mod=n, expert_rows=16, s5_chunk=t)


def _trunk(x, mod, p, past, prompt):
    bsz, t, d = x.shape
    n = bsz * t
    depth = p['norm1'].shape[0]
    aw = d // 2
    n_heads = aw // HEAD_DIM
    tiles = _tiles(prompt, n, t)
    tm, rows_per_mod, tm_e = tiles.rows, tiles.rows_per_mod, tiles.expert_rows
    xf = x.reshape(n, d)
    ab_states, cd_states = [], []
    for l in range(depth):
        i = l // 2
        m6 = mod[l].reshape(bsz, 6, d).transpose(1, 0, 2)
        if prompt:
            mod6 = m6[:, :, None, :]
        else:
            mod6 = jnp.repeat(m6, t, axis=1)[:, None, :, :]
        if l % 2 == 0:
            proj = norm_proj(xf, p['norm1'][l], mod6, 1, 0, _ab_weight(p['w_in_ab'][i]), tm, rows_per_mod)
            k = proj[:, aw:2 * aw]
            v = proj[:, 2 * aw:3 * aw]
            ikiw = proj[:, 7 * aw:7 * aw + LANES]
            ik = ikiw[:, :IDX_DIM]
            if prompt:
                zeros = jnp.zeros_like(ik)
                ik_e = _split_stack(jnp.concatenate([ik, zeros], axis=1))
                ik_o = _split_stack(jnp.concatenate([zeros, ik], axis=1))
                a = dsa_prompt(proj, ik_e, ik_o, n_heads, bsz, t, tiles.dsa_q, tiles.dsa_k)
                b, nbuf = conv_prompt(proj, p['conv_w'][i], bsz, t, tiles.conv_rows)
                new_buf = nbuf[:, 8 - (CONV_W - 1):]
            else:
                a = dsa_sample(proj, ikiw, past['cache_a_k'], past['cache_a_v'], past['cache_a_kidx'],
                               past['page_table'], i, bsz, t)
                b, u = conv_sample(proj, p['conv_w'][i], past['state_b_conv'][i], t)
                new_buf = u.reshape(bsz, t, aw)[:, t - (CONV_W - 1):]
            xf = out_proj(a, b, p['w_out_ab'][i], xf, mod6, 2, tm, rows_per_mod)
            ab_states.append((k.reshape(bsz, t, n_heads, HEAD_DIM), v.reshape(bsz, t, n_heads, HEAD_DIM),
                              ik.reshape(bsz, t, IDX_DIM), new_buf))
        else:
            proj = norm_proj(xf, p['norm1'][l], mod6, 1, 0, _split_stack(p['w_in_cd'][i]), tm, rows_per_mod)
            k = proj[:, 2 * aw:3 * aw]
            v = proj[:, 3 * aw:4 * aw]
            chunk = tiles.s5_chunk
            wts = s5_weights(p['s5_log_dt'][i], p['s5_a_re'][i], p['s5_a_im'][i], p['s5_b_re'][i], p['s5_b_im'][i],
                             p['s5_c_re'][i], p['s5_c_im'][i], chunk)
            nblk = aw // LANES
            if prompt:
                h0 = jnp.zeros((nblk, bsz, 1, (aw // S5_GROUP) * S5_STATE // nblk), F32)
                y, hr, hi = s5_prompt(proj, h0, h0, wts, p['s5_d'][i], bsz, t, chunk, tiles.s5_chunks)
                hr, hi = hr[:, :, 0], hi[:, :, 0]
                dd = moba_prompt(proj, bsz, t)
            else:
                to_blk = lambda s: s.reshape(bsz, nblk, -1).transpose(1, 0, 2)
                y, hr, hi = s5_sample(proj, to_blk(past['state_c_re'][i]), to_blk(past['state_c_im'][i]), wts,
                                      p['s5_d'][i], bsz, chunk)
                dd = moba_sample(proj, past['cache_d_k'], past['cache_d_v'], past['page_table'], i, bsz, t)
            from_blk = lambda s: s.transpose(1, 0, 2).reshape(bsz, aw // S5_GROUP, S5_STATE)
            c = glu(y, p['glu_w'][i], p['glu_b'][i], tm)
            xf = out_proj(c, dd, p['w_out_cd'][i], xf, mod6, 2, tm, rows_per_mod)
            cd_states.append((from_blk(hr), from_blk(hi), k.reshape(bsz, t, n_heads, HEAD_DIM),
                              v.reshape(bsz, t, n_heads, HEAD_DIM)))
        w_r, b_r = _router_weight(p['router_g_w'][l], p['router_g_b'][l], p['router_e_w'][l], p['router_e_b'][l])
        xf = hier_moe(xf, p['norm2'][l], mod6, w_r, b_r, p['exp_w1'], p['exp_w3'], p['exp_w2'], l, tm,
                      rows_per_mod, tm_e)
    y = final_norm(xf, p['final_norm'], tm).reshape(bsz, t, d)
    a_k, a_v, a_kidx, b_conv = [jnp.stack(s) for s in zip(*ab_states)]
    c_re, c_im, d_k, d_v = [jnp.stack(s) for s in zip(*cd_states)]
    return y, a_k, a_v, a_kidx, b_conv, c_re, c_im, d_k, d_v


def kernel(x_prompt, x_sample, cache_a_k, cache_a_v, cache_a_kidx, state_b_conv, state_c_re, state_c_im, cache_d_k, cache_d_v, page_table, c_prompt, c_sample, norm1, norm2, ada_w, ada_b, w_in_ab, conv_w, w_out_ab, w_in_cd, s5_a_re, s5_a_im, s5_b_re, s5_b_im, s5_c_re, s5_c_im, s5_d, s5_log_dt, glu_w, glu_b, w_out_cd, router_g_w, router_g_b, router_e_w, router_e_b, exp_w1, exp_w3, exp_w2, final_norm):
    p = dict(norm1=norm1, norm2=norm2, w_in_ab=w_in_ab, conv_w=conv_w, w_out_ab=w_out_ab, w_in_cd=w_in_cd,
             s5_a_re=s5_a_re, s5_a_im=s5_a_im, s5_b_re=s5_b_re, s5_b_im=s5_b_im, s5_c_re=s5_c_re, s5_c_im=s5_c_im,
             s5_d=s5_d, s5_log_dt=s5_log_dt, glu_w=glu_w, glu_b=glu_b, w_out_cd=w_out_cd, router_g_w=router_g_w,
             router_g_b=router_g_b, router_e_w=router_e_w, router_e_b=router_e_b, exp_w1=exp_w1, exp_w3=exp_w3,
             exp_w2=exp_w2, final_norm=final_norm)
    past = dict(cache_a_k=cache_a_k, cache_a_v=cache_a_v, cache_a_kidx=cache_a_kidx, state_b_conv=state_b_conv,
                state_c_re=state_c_re, state_c_im=state_c_im, cache_d_k=cache_d_k, cache_d_v=cache_d_v,
                page_table=page_table)
    bp, bs = c_prompt.shape[0], c_sample.shape[0]
    rows = -(-(bp + bs) // 8) * 8
    c_all = jnp.concatenate([c_prompt, c_sample, jnp.zeros((rows - bp - bs, c_prompt.shape[1]), F32)], axis=0)
    mod = ada_mod(c_all, ada_w, ada_b)
    outs_p = _trunk(x_prompt, mod[:, :bp], p, None, True)
    outs_s = _trunk(x_sample, mod[:, bp:bp + bs], p, past, False)
    return (outs_p[0], outs_s[0]) + tuple(outs_p[1:]) + tuple(outs_s[1:])
```
